```python
import math
import jax
import jax.numpy as jnp
from jax import lax
import numpy as np


D_MODEL = 1024
BATCH = 8
SEQ = 4096
DEPTH = 2
DEC_BATCH = 32
DEC_SEQ = 2048
PAST_LEN = 128

GRID_W = 64
N_EVEN = (DEPTH + 1) // 2
N_ODD = DEPTH // 2

S5_WIDTH = D_MODEL // 2
S5_GROUP = 16
S5_GROUPS = S5_WIDTH // S5_GROUP
S5_STATE = 64
DT_MIN = 1e-3
DT_MAX = 1e-1

NA_HEAD_DIM = 64
NA_HEADS = (D_MODEL // 2) // NA_HEAD_DIM
NA_WIDTH = NA_HEADS * NA_HEAD_DIM
NA_ROWS_MAX = 8
NA_COLS = 16

EVEN_IN = S5_WIDTH + 3 * NA_WIDTH
EVEN_MIX = S5_WIDTH + NA_WIDTH

GQA_HEAD_DIM = 64
GQA_HEADS = D_MODEL // GQA_HEAD_DIM
GQA_KV_HEADS = GQA_HEADS // 4
GQA_GROUP = GQA_HEADS // GQA_KV_HEADS
WINDOW = 128
BLOCK = 128
ODD_IN = (GQA_HEADS + 2 * GQA_KV_HEADS) * GQA_HEAD_DIM
ODD_MIX = GQA_HEADS * GQA_HEAD_DIM

T5_BUCKETS = 32
T5_MAX_DIST = 128

D_FF = 2816
RMS_EPS = 1e-6
NEG_INF = -1e30

kernel_name = 'hybrid_s5_natten_swa_encoder'


def rms_norm(x, g):
    xf = x.astype(jnp.float32)
    y = xf * lax.rsqrt(jnp.mean(xf * xf, axis=-1, keepdims=True) + RMS_EPS)
    return (y * g.astype(jnp.float32)).astype(x.dtype)


def swiglu(x, wg, wu, wd):
    return (jax.nn.silu(x @ wg) * (x @ wu)) @ wd


def _cmul(ar, ai, br, bi):
    return ar * br - ai * bi, ar * bi + ai * br


def _ssm_combine(e1, e2):
    a1r, a1i, b1r, b1i = e1
    a2r, a2i, b2r, b2i = e2
    ar, ai = _cmul(a2r, a2i, a1r, a1i)
    br, bi = _cmul(a2r, a2i, b1r, b1i)
    return ar, ai, br + b2r, bi + b2i


def s5_scan(ug, lam_re, lam_im, log_dt, b_re, b_im, c_re, c_im, reverse):
    dt = jnp.exp(log_dt)[:, None]
    mag = jnp.exp(lam_re * dt)
    ab_re = mag * jnp.cos(lam_im * dt)
    ab_im = mag * jnp.sin(lam_im * dt)
    den = lam_re * lam_re + lam_im * lam_im
    nr = ab_re - 1.0
    cr = (nr * lam_re + ab_im * lam_im) / den
    ci = (ab_im * lam_re - nr * lam_im) / den
    bb_re = cr[..., None] * b_re - ci[..., None] * b_im
    bb_im = cr[..., None] * b_im + ci[..., None] * b_re
    bu_re = jnp.einsum('blgc,gpc->blgp', ug, bb_re)
    bu_im = jnp.einsum('blgc,gpc->blgp', ug, bb_im)
    a_re = jnp.broadcast_to(ab_re, bu_re.shape)
    a_im = jnp.broadcast_to(ab_im, bu_im.shape)
    _, _, s_re, s_im = lax.associative_scan(_ssm_combine, (a_re, a_im, bu_re, bu_im), reverse=reverse, axis=1)
    return jnp.einsum('blgp,gcp->blgc', s_re, c_re) - jnp.einsum('blgp,gcp->blgc', s_im, c_im)


def s5_mixer(u, lam_re, lam_im, log_dt, b_re, b_im, c_re, c_im, d_skip, w_glu, b_glu):
    bsz, seq = u.shape[0], u.shape[1]
    f32 = jnp.float32
    uf = u.astype(f32)
    ug = uf.reshape(bsz, seq, S5_GROUPS, S5_GROUP)
    y = d_skip.astype(f32) * uf
    for direction in range(2):
        yd = s5_scan(ug, lam_re[direction].astype(f32), lam_im[direction].astype(f32),
                     log_dt[direction].astype(f32), b_re[direction].astype(f32), b_im[direction].astype(f32),
                     c_re[direction].astype(f32), c_im[direction].astype(f32), reverse=(direction == 1))
        y = y + yd.reshape(bsz, seq, S5_WIDTH)
    g = jax.nn.gelu(y)
    out = g * jax.nn.sigmoid(g @ w_glu.astype(f32) + b_glu.astype(f32))
    return out.astype(u.dtype)


def neighbourhood_attention(q, k, v, rpb):
    bsz, seq, h, dh = q.shape
    rows = seq // GRID_W
    kh = min(NA_ROWS_MAX, rows)
    kw = NA_COLS
    f32 = jnp.float32
    qg = q.reshape(bsz, rows, GRID_W, h, dh)
    kg = k.reshape(bsz, rows, GRID_W, h, dh)
    vg = v.reshape(bsz, rows, GRID_W, h, dh)
    row_start = jnp.clip(jnp.arange(rows) - kh // 2, 0, rows - kh)
    rel_row = row_start[:, None] + jnp.arange(kh)[None, :] - jnp.arange(rows)[:, None]
    col_idx = jnp.clip(jnp.arange(GRID_W) - kw // 2, 0, GRID_W - kw)[:, None] + jnp.arange(kw)[None, :]
    rel_col = col_idx - jnp.arange(GRID_W)[:, None]
    scale = dh ** -0.5

    def one_row(args):
        q_r, rs, rr = args
        k_win = lax.dynamic_slice_in_dim(kg, rs, kh, axis=1)[:, :, col_idx]
        v_win = lax.dynamic_slice_in_dim(vg, rs, kh, axis=1)[:, :, col_idx]
        bias = rpb[:, rr[:, None, None] + NA_ROWS_MAX - 1, rel_col[None] + NA_COLS - 1]
        bias = jnp.transpose(bias, (2, 0, 1, 3)).astype(f32)
        s = jnp.einsum('bqhd,bkqjhd->bqhkj', q_r, k_win).astype(f32) * scale + bias[None]
        p = jax.nn.softmax(s.reshape(bsz, GRID_W, h, kh * kw), axis=-1).reshape(s.shape)
        return jnp.einsum('bqhkj,bkqjhd->bqhd', p.astype(v.dtype), v_win)

    out = lax.map(one_row, (jnp.moveaxis(qg, 1, 0), row_start, rel_row))
    return jnp.moveaxis(out, 0, 1).reshape(bsz, seq, h * dh)


def t5_bucket(rel):
    half = T5_BUCKETS // 2
    max_exact = half // 2
    ret = jnp.where(rel > 0, half, 0)
    n = jnp.abs(rel)
    nf = jnp.maximum(n, 1).astype(jnp.float32)
    large = max_exact + (jnp.log(nf / max_exact) / math.log(T5_MAX_DIST / max_exact)
                         * (half - max_exact)).astype(jnp.int32)
    large = jnp.minimum(large, half - 1)
    return ret + jnp.where(n < max_exact, n, large)


def windowed_gqa(q, k, v, sink, t5_table):
    bsz, seq = q.shape[0], q.shape[1]
    nb = seq // BLOCK
    f32 = jnp.float32
    qb = jnp.moveaxis(q.reshape(bsz, nb, BLOCK, GQA_KV_HEADS, GQA_GROUP, GQA_HEAD_DIM), 1, 0)
    pad = ((0, 0), (BLOCK, BLOCK), (0, 0), (0, 0))
    kp = jnp.pad(k, pad)
    vp = jnp.pad(v, pad)
    qi = jnp.arange(BLOCK)[:, None]
    kj = jnp.arange(3 * BLOCK)[None, :]
    rel = kj - BLOCK - qi
    bias = jnp.transpose(t5_table[t5_bucket(rel)], (2, 0, 1)).astype(f32)
    bias = bias.reshape(GQA_KV_HEADS, GQA_GROUP, BLOCK, 3 * BLOCK)
    in_window = jnp.abs(rel) <= WINDOW
    sk = sink.astype(f32).reshape(GQA_KV_HEADS, GQA_GROUP)[None, :, :, None, None]
    scale = GQA_HEAD_DIM ** -0.5

    def one_block(args):
        q_n, n = args
        start = n * BLOCK
        k_n = lax.dynamic_slice_in_dim(kp, start, 3 * BLOCK, axis=1)
        v_n = lax.dynamic_slice_in_dim(vp, start, 3 * BLOCK, axis=1)
        kpos = start + kj - BLOCK
        valid = in_window & (kpos >= 0) & (kpos < seq)
        s = jnp.einsum('bqgrd,bkgd->bgrqk', q_n, k_n).astype(f32) * scale + bias[None]
        s = jnp.where(valid, s, NEG_INF)
        m = jnp.maximum(jnp.max(s, axis=-1, keepdims=True), sk)
        p = jnp.exp(s - m)
        denom = jnp.sum(p, axis=-1, keepdims=True) + jnp.exp(sk - m)
        return jnp.einsum('bgrqk,bkgd->bqgrd', (p / denom).astype(v.dtype), v_n)

    out = lax.map(one_block, (qb, jnp.arange(nb)))
    return jnp.moveaxis(out, 0, 1).reshape(bsz, seq, ODD_MIX)


def trunk(x, norm_ffn, w_ffn_gate, w_ffn_up, w_ffn_down, norm_mix, w_in_even,
          s5_lam_re, s5_lam_im, s5_log_dt, s5_b_re, s5_b_im, s5_c_re, s5_c_im,
          s5_d, s5_w_glu, s5_b_glu, na_rpb, w_out_even, w_in_odd, gqa_sink, w_out_odd,
          t5_table, norm_final):
    bsz, seq = x.shape[0], x.shape[1]
    for layer in range(DEPTH):
        i = layer // 2
        x = x + 0.5 * swiglu(rms_norm(x, norm_ffn[layer, 0]), w_ffn_gate[layer, 0],
                             w_ffn_up[layer, 0], w_ffn_down[layer, 0])
        hn = rms_norm(x, norm_mix[layer])
        if layer % 2 == 0:
            z = hn @ w_in_even[i]
            u = z[..., :S5_WIDTH]
            q, k, v = jnp.split(z[..., S5_WIDTH:], 3, axis=-1)
            q = q.reshape(bsz, seq, NA_HEADS, NA_HEAD_DIM)
            k = k.reshape(bsz, seq, NA_HEADS, NA_HEAD_DIM)
            v = v.reshape(bsz, seq, NA_HEADS, NA_HEAD_DIM)
            y_a = s5_mixer(u, s5_lam_re[i], s5_lam_im[i], s5_log_dt[i], s5_b_re[i], s5_b_im[i],
                           s5_c_re[i], s5_c_im[i], s5_d[i], s5_w_glu[i], s5_b_glu[i])
            y_b = neighbourhood_attention(q, k, v, na_rpb[i])
            x = x + jnp.concatenate([y_a, y_b], axis=-1) @ w_out_even[i]
        else:
            z = hn @ w_in_odd[i]
            nq = GQA_HEADS * GQA_HEAD_DIM
            nkv = GQA_KV_HEADS * GQA_HEAD_DIM
            q = z[..., :nq].reshape(bsz, seq, GQA_HEADS, GQA_HEAD_DIM)
            k = z[..., nq:nq + nkv].reshape(bsz, seq, GQA_KV_HEADS, GQA_HEAD_DIM)
            v = z[..., nq + nkv:].reshape(bsz, seq, GQA_KV_HEADS, GQA_HEAD_DIM)
            x = x + windowed_gqa(q, k, v, gqa_sink[i], t5_table) @ w_out_odd[i]
        x = x + 0.5 * swiglu(rms_norm(x, norm_ffn[layer, 1]), w_ffn_gate[layer, 1],
                             w_ffn_up[layer, 1], w_ffn_down[layer, 1])
    return rms_norm(x, norm_final)


def setup_inputs(seed: int = 0) -> dict:
    key = jax.random.key(seed)
    ks = jax.random.split(key, 32)
    f = jnp.float32

    def nrm(k, shape, scale):
        return jax.random.normal(k, shape, f) * scale

    lam_im_base = jnp.pi * jnp.arange(S5_STATE, dtype=f)
    return {
        'x_prompt': nrm(ks[0], (BATCH, SEQ, D_MODEL), 1.0),
        'x_sample': nrm(ks[1], (DEC_BATCH, DEC_SEQ, D_MODEL), 1.0),
        'norm_ffn': 1.0 + nrm(ks[2], (DEPTH, 2, D_MODEL), 0.02),
        'w_ffn_gate': nrm(ks[3], (DEPTH, 2, D_MODEL, D_FF), D_MODEL ** -0.5),
        'w_ffn_up': nrm(ks[4], (DEPTH, 2, D_MODEL, D_FF), D_MODEL ** -0.5),
        'w_ffn_down': nrm(ks[5], (DEPTH, 2, D_FF, D_MODEL), D_FF ** -0.5),
        'norm_mix': 1.0 + nrm(ks[6], (DEPTH, D_MODEL), 0.02),
        'w_in_even': nrm(ks[7], (N_EVEN, D_MODEL, EVEN_IN), D_MODEL ** -0.5),
        's5_lam_re': -0.5 + nrm(ks[8], (N_EVEN, 2, S5_GROUPS, S5_STATE), 0.01),
        's5_lam_im': lam_im_base + nrm(ks[9], (N_EVEN, 2, S5_GROUPS, S5_STATE), 0.01),
        's5_log_dt': jax.random.uniform(ks[10], (N_EVEN, 2, S5_GROUPS), f,
                                        minval=math.log(DT_MIN), maxval=math.log(DT_MAX)),
        's5_b_re': nrm(ks[11], (N_EVEN, 2, S5_GROUPS, S5_STATE, S5_GROUP), S5_GROUP ** -0.5),
        's5_b_im': nrm(ks[12], (N_EVEN, 2, S5_GROUPS, S5_STATE, S5_GROUP), S5_GROUP ** -0.5),
        's5_c_re': nrm(ks[13], (N_EVEN, 2, S5_GROUPS, S5_GROUP, S5_STATE), S5_STATE ** -0.5),
        's5_c_im': nrm(ks[14], (N_EVEN, 2, S5_GROUPS, S5_GROUP, S5_STATE), S5_STATE ** -0.5),
        's5_d': nrm(ks[15], (N_EVEN, S5_WIDTH), 1.0),
        's5_w_glu': nrm(ks[16], (N_EVEN, S5_WIDTH, S5_WIDTH), S5_WIDTH ** -0.5),
        's5_b_glu': nrm(ks[17], (N_EVEN, S5_WIDTH), 0.01),
        'na_rpb': nrm(ks[18], (N_EVEN, NA_HEADS, 2 * NA_ROWS_MAX - 1, 2 * NA_COLS - 1), 0.1),
        'w_out_even': nrm(ks[19], (N_EVEN, EVEN_MIX, D_MODEL), EVEN_MIX ** -0.5),
        'w_in_odd': nrm(ks[20], (N_ODD, D_MODEL, ODD_IN), D_MODEL ** -0.5),
        'gqa_sink': nrm(ks[21], (N_ODD, GQA_HEADS), 0.5),
        'w_out_odd': nrm(ks[22], (N_ODD, ODD_MIX, D_MODEL), ODD_MIX ** -0.5),
        't5_table': nrm(ks[23], (T5_BUCKETS, GQA_HEADS), 0.5),
        'norm_final': 1.0 + nrm(ks[24], (D_MODEL,), 0.02),
    }


def reference(x_prompt, x_sample, norm_ffn, w_ffn_gate, w_ffn_up, w_ffn_down, norm_mix, w_in_even,
              s5_lam_re, s5_lam_im, s5_log_dt, s5_b_re, s5_b_im, s5_c_re, s5_c_im,
              s5_d, s5_w_glu, s5_b_glu, na_rpb, w_out_even, w_in_odd, gqa_sink, w_out_odd,
              t5_table, norm_final):
    y_prompt = trunk(x_prompt, norm_ffn, w_ffn_gate, w_ffn_up, w_ffn_down, norm_mix, w_in_even,
                     s5_lam_re, s5_lam_im, s5_log_dt, s5_b_re, s5_b_im, s5_c_re, s5_c_im,
                     s5_d, s5_w_glu, s5_b_glu, na_rpb, w_out_even, w_in_odd, gqa_sink, w_out_odd,
                     t5_table, norm_final)
    y_sample = trunk(x_sample, norm_ffn, w_ffn_gate, w_ffn_up, w_ffn_down, norm_mix, w_in_even,
                     s5_lam_re, s5_lam_im, s5_log_dt, s5_b_re, s5_b_im, s5_c_re, s5_c_im,
                     s5_d, s5_w_glu, s5_b_glu, na_rpb, w_out_even, w_in_odd, gqa_sink, w_out_odd,
                     t5_table, norm_final)
    return (y_prompt, y_sample)
```

```python
import functools
import math

import jax
import jax.numpy as jnp
import numpy as np
from jax import lax
from jax.experimental import pallas as pl
from jax.experimental.pallas import tpu as pltpu

F32 = jnp.float32
BF16 = jnp.bfloat16

S5_GROUP = 16
S5_STATE = 64
NA_HEAD_DIM = 64
NA_ROWS = 8
NA_COLS = 16
GRID_W = 64
GQA_HEAD_DIM = 64
GQA_GROUP = 4
WINDOW = 128
BLOCK = 128
T5_BUCKETS = 32
T5_MAX_DIST = 128
RMS_EPS = 1e-6
NEG_INF = -1e30

CHUNK = 64
CHUNK_W = CHUNK * S5_GROUP
LANES = 128
TOKEN_TILE = 512
FFN_TILE = 1408
VMEM_LIMIT = 48 * 1024 * 1024

_NT = (((1,), (1,)), ((), ()))
_HI = lax.Precision.HIGHEST


def _params(n_axes):
    return pltpu.CompilerParams(dimension_semantics=("arbitrary",) * n_axes,
                                vmem_limit_bytes=VMEM_LIMIT)


def _rms(x, g):
    return x * lax.rsqrt(jnp.mean(x * x, axis=-1, keepdims=True) + RMS_EPS) * g


def _dot(a, b):
    return jnp.dot(a, b, preferred_element_type=F32)


def _dot_nt(a, b, precision=None):
    return lax.dot_general(a, b, _NT, precision=precision, preferred_element_type=F32)


def _ffn_kernel(x_ref, g_ref, wg_ref, wu_ref, wd_ref, gf_ref, o_ref, hn_ref, acc_ref, *, final_norm):
    j = pl.program_id(1)

    @pl.when(j == 0)
    def _():
        hn_ref[...] = _rms(x_ref[...], g_ref[...]).astype(BF16)
        acc_ref[...] = jnp.zeros_like(acc_ref)

    hn = hn_ref[...]
    gate = _dot(hn, wg_ref[...])
    up = _dot(hn, wu_ref[...])
    h = (gate * jax.nn.sigmoid(gate) * up).astype(BF16)
    acc_ref[...] += _dot(h, wd_ref[...])

    @pl.when(j == pl.num_programs(1) - 1)
    def _():
        y = x_ref[...] + 0.5 * acc_ref[...]
        if final_norm:
            y = _rms(y, gf_ref[...])
        o_ref[...] = y


def _ffn(x, g, wg, wu, wd, g_final, final_norm):
    t, d = x.shape
    f = wg.shape[1]
    tm, tf = TOKEN_TILE, FFN_TILE
    return pl.pallas_call(
        functools.partial(_ffn_kernel, final_norm=final_norm),
        grid=(t // tm, f // tf),
        in_specs=[
            pl.BlockSpec((tm, d), lambda i, j: (i, 0)),
            pl.BlockSpec((1, d), lambda i, j: (0, 0)),
            pl.BlockSpec((d, tf), lambda i, j: (0, j)),
            pl.BlockSpec((d, tf), lambda i, j: (0, j)),
            pl.BlockSpec((tf, d), lambda i, j: (j, 0)),
            pl.BlockSpec((1, d), lambda i, j: (0, 0)),
        ],
        out_specs=pl.BlockSpec((tm, d), lambda i, j: (i, 0)),
        out_shape=jax.ShapeDtypeStruct((t, d), F32),
        scratch_shapes=[pltpu.VMEM((tm, d), BF16), pltpu.VMEM((tm, d), F32)],
        compiler_params=_params(2),
    )(x, g, wg, wu, wd, g_final)


def _inproj_kernel(x_ref, g_ref, w_ref, *o_refs, scales):
    hn = _rms(x_ref[...], g_ref[...]).astype(BF16)
    off = 0
    for o_ref, scale in zip(o_refs, scales):
        width = o_ref.shape[1]
        part = _dot(hn, w_ref[:, off:off + width])
        if scale != 1.0:
            part = part * scale
        o_ref[...] = part.astype(o_ref.dtype)
        off += width


def _inproj(x, g, w, splits):
    t, d = x.shape
    n = w.shape[1]
    tm = TOKEN_TILE
    return pl.pallas_call(
        functools.partial(_inproj_kernel, scales=tuple(s for _, _, s in splits)),
        grid=(t // tm,),
        in_specs=[
            pl.BlockSpec((tm, d), lambda i: (i, 0)),
            pl.BlockSpec((1, d), lambda i: (0, 0)),
            pl.BlockSpec((d, n), lambda i: (0, 0)),
        ],
        out_specs=[pl.BlockSpec((tm, width), lambda i: (i, 0)) for width, _, _ in splits],
        out_shape=[jax.ShapeDtypeStruct((t, width), dt) for width, dt, _ in splits],
        compiler_params=_params(1),
    )(x, g, w)


def _cmul(ar, ai, br, bi):
    return ar * br - ai * bi, ar * bi + ai * br


def _s5_table_kernel(lre_ref, lim_ref, ldt_ref, btr_ref, bti_ref, cr_ref, ci_ref,
                     m_ref, zb_ref, oc_ref, aq_ref):
    q = CHUNK
    p = S5_STATE
    c = S5_GROUP
    two_q = 2 * q
    eye = (lax.broadcasted_iota(jnp.int32, (p, p), 0) == lax.broadcasted_iota(jnp.int32, (p, p), 1)).astype(F32)

    def expand(n_of_row, table):
        rows = n_of_row.shape[0]
        sel = (n_of_row == lax.broadcasted_iota(jnp.int32, (rows, two_q), 1)).astype(F32)
        return jnp.dot(sel, table, precision=_HI, preferred_element_type=F32)

    def tile_rows(a, reps):
        return jnp.concatenate([a] * reps, axis=0)

    row2 = lax.broadcasted_iota(jnp.int32, (two_q * c, 1), 0) // c
    row1 = lax.broadcasted_iota(jnp.int32, (q * c, 1), 0) // c

    kernel_t = None
    for d in range(2):
        lr = lre_ref[d, 0]
        li = lim_ref[d, 0]
        dt = jnp.exp(ldt_ref[d, 0])
        mag = jnp.exp(lr * dt)
        ar = mag * jnp.cos(li * dt)
        ai = mag * jnp.sin(li * dt)
        den = lr * lr + li * li
        nr = ar - 1.0
        zr = (nr * lr + ai * li) / den
        zi = (ai * lr - nr * li) / den
        btr = btr_ref[d, 0]
        bti = bti_ref[d, 0]
        bbr = zr * btr - zi * bti
        bbi = zr * bti + zi * btr
        cre = cr_ref[d, 0]
        cim = ci_ref[d, 0]

        n_col = lax.broadcasted_iota(jnp.int32, (two_q, 1), 0)
        pr = jnp.ones((two_q, p), F32)
        pi = jnp.zeros((two_q, p), F32)
        sr, si = ar, ai
        for j in range(two_q.bit_length() - 1):
            bit = ((n_col >> j) & 1) == 1
            mr, mi = _cmul(pr, pi, sr, si)
            pr = jnp.where(bit, mr, pr)
            pi = jnp.where(bit, mi, pi)
            sr, si = _cmul(sr, si, sr, si)
        aq_re = pr[q:q + 1]
        aq_im = pi[q:q + 1]

        lag = (row2 - q) if d == 0 else (q - row2)
        er, ei = expand(lag, pr), expand(lag, pi)
        car, cai = _cmul(tile_rows(cre, two_q), tile_rows(cim, two_q), er, ei)
        kt = _dot_nt(bbr, car, _HI) - _dot_nt(bbi, cai, _HI)
        kernel_t = kt if kernel_t is None else kernel_t + kt

        n_z = (q - 1 - row1) if d == 0 else row1
        zbr, zbi = _cmul(tile_rows(bbr, q), tile_rows(bbi, q), expand(n_z, pr), expand(n_z, pi))
        zb_ref[0, :, d * p:(d + 1) * p] = zbr.astype(BF16)
        zb_ref[0, :, 2 * p + d * p:2 * p + (d + 1) * p] = zbi.astype(BF16)

        n_o = (row1 + 1) if d == 0 else (q - row1)
        e_re, e_im = _cmul(tile_rows(cre, q), tile_rows(cim, q), expand(n_o, pr), expand(n_o, pi))
        oc_ref[0, d * p:(d + 1) * p, :] = _dot_nt(eye, e_re, _HI).astype(BF16)
        oc_ref[0, 2 * p + d * p:2 * p + (d + 1) * p, :] = (-_dot_nt(eye, e_im, _HI)).astype(BF16)

        aq_ref[0, :, d * p:(d + 1) * p] = aq_re
        aq_ref[0, :, 2 * p + d * p:2 * p + (d + 1) * p] = aq_im

    per_vreg = LANES // c
    width = two_q * c
    for rho in range(per_vreg):
        rolled = kernel_t if rho == 0 else pltpu.roll(kernel_t, width - c * rho, axis=1)
        for m in range(q // per_vreg):
            s = (per_vreg - rho) % per_vreg + per_vreg * m
            start = (q - s - rho) // per_vreg * LANES
            m_ref[0, s * c:(s + 1) * c, :] = rolled[:, start:start + q * c].astype(BF16)


def _s5_tables(lam_re, lam_im, log_dt, b_re, b_im, c_re, c_im):
    g = lam_re.shape[1]
    p, c, q = S5_STATE, S5_GROUP, CHUNK
    vec = lambda a: a.reshape(2, g, 1, a.shape[-1] if a.ndim == 3 else 1)
    swap = lambda a: jnp.swapaxes(a, -1, -2)
    spec4 = lambda r, w: pl.BlockSpec((2, 1, r, w), lambda i: (0, i, 0, 0))
    return pl.pallas_call(
        _s5_table_kernel,
        grid=(g,),
        in_specs=[spec4(1, p), spec4(1, p), spec4(1, 1), spec4(c, p), spec4(c, p), spec4(c, p), spec4(c, p)],
        out_specs=[
            pl.BlockSpec((1, q * c, q * c), lambda i: (i, 0, 0)),
            pl.BlockSpec((1, q * c, 4 * p), lambda i: (i, 0, 0)),
            pl.BlockSpec((1, 4 * p, q * c), lambda i: (i, 0, 0)),
            pl.BlockSpec((1, 1, 4 * p), lambda i: (i, 0, 0)),
        ],
        out_shape=[
            jax.ShapeDtypeStruct((g, q * c, q * c), BF16),
            jax.ShapeDtypeStruct((g, q * c, 4 * p), BF16),
            jax.ShapeDtypeStruct((g, 4 * p, q * c), BF16),
            jax.ShapeDtypeStruct((g, 1, 4 * p), F32),
        ],
        compiler_params=_params(1),
    )(vec(lam_re), vec(lam_im), vec(log_dt), swap(b_re), swap(b_im), c_re, c_im)


def _s5_kernel(u_ref, m_ref, zb_ref, oc_ref, aq_ref, y_ref, z_scr, sf_scr, sb_scr, *, nc, bsz):
    half = 2 * S5_STATE
    u = u_ref[0]
    z_scr[...] = _dot(u, zb_ref[0])
    aq = aq_ref[0]
    ar, ai = aq[:, :half], aq[:, half:]
    zero = jnp.zeros((bsz, half), F32)

    def step(j, carry):
        fr, fi, br, bi = carry
        rf = pl.multiple_of(j * bsz, 8)
        rb = pl.multiple_of((nc - 1 - j) * bsz, 8)
        sf_scr[pl.ds(rf, bsz), :half] = fr
        sf_scr[pl.ds(rf, bsz), half:] = fi
        sb_scr[pl.ds(rb, bsz), :half] = br
        sb_scr[pl.ds(rb, bsz), half:] = bi
        zf = z_scr[pl.ds(rf, bsz), :]
        zb = z_scr[pl.ds(rb, bsz), :]
        nfr, nfi = _cmul(ar, ai, fr, fi)
        nbr, nbi = _cmul(ar, ai, br, bi)
        return nfr + zf[:, :half], nfi + zf[:, half:], nbr + zb[:, :half], nbi + zb[:, half:]

    lax.fori_loop(0, nc, step, (zero, zero, zero, zero))
    lane = lax.broadcasted_iota(jnp.int32, (1, 2 * half), 1)
    is_fwd = (lane & S5_STATE) == 0
    s = jnp.where(is_fwd, sf_scr[...], sb_scr[...]).astype(BF16)
    y_ref[0] = _dot(u, m_ref[0]) + _dot(s, oc_ref[0])


def _s5_apply(u_cm, m, zb, oc, aq, nc, bsz):
    g, rows, w = u_cm.shape
    st = zb.shape[2]
    return pl.pallas_call(
        functools.partial(_s5_kernel, nc=nc, bsz=bsz),
        grid=(g,),
        in_specs=[
            pl.BlockSpec((1, rows, w), lambda i: (i, 0, 0)),
            pl.BlockSpec((1, w, w), lambda i: (i, 0, 0)),
            pl.BlockSpec((1, w, st), lambda i: (i, 0, 0)),
            pl.BlockSpec((1, st, w), lambda i: (i, 0, 0)),
            pl.BlockSpec((1, 1, st), lambda i: (i, 0, 0)),
        ],
        out_specs=pl.BlockSpec((1, rows, w), lambda i: (i, 0, 0)),
        out_shape=jax.ShapeDtypeStruct((g, rows, w), F32),
        scratch_shapes=[pltpu.VMEM((rows, st), F32)] * 3,
        compiler_params=_params(1),
    )(u_cm, m, zb, oc, aq)


def _na_bias_kernel(rpb_ref, o_ref):
    st = pl.program_id(0)
    h = pl.program_id(1)
    n_rel_rows = 2 * NA_ROWS - 1
    n_rel_cols = 2 * NA_COLS - 1
    w = lax.broadcasted_iota(jnp.int32, (GRID_W, LANES), 0)
    lane = lax.broadcasted_iota(jnp.int32, (GRID_W, LANES), 1)
    wk = lane & (GRID_W - 1)
    first = lane < GRID_W
    cs = jnp.clip(w - NA_COLS // 2, 0, GRID_W - NA_COLS)
    valid = (wk >= cs) & (wk < cs + NA_COLS)
    rel = wk - w + NA_COLS - 1
    for i in range(NA_ROWS * GRID_W // LANES):
        base0 = (h * n_rel_rows + st + 2 * i) * n_rel_cols
        base1 = base0 + n_rel_cols
        acc = jnp.full((GRID_W, LANES), NEG_INF, F32)
        for j in range(n_rel_cols):
            val = jnp.where(first, rpb_ref[base0 + j], rpb_ref[base1 + j])
            acc = jnp.where(valid & (rel == j), val, acc)
        o_ref[0, 0, :, i * LANES:(i + 1) * LANES] = acc


def _na_bias(rpb):
    h = rpb.shape[0]
    pairs = h // 2
    return pl.pallas_call(
        _na_bias_kernel,
        grid=(NA_ROWS, h),
        in_specs=[pl.BlockSpec(memory_space=pltpu.SMEM)],
        out_specs=pl.BlockSpec((1, 1, GRID_W, NA_ROWS * GRID_W), lambda s, i: (s, i // 2, i % 2, 0)),
        out_shape=jax.ShapeDtypeStruct((NA_ROWS, pairs, 2 * GRID_W, NA_ROWS * GRID_W), F32),
        compiler_params=_params(2),
    )(rpb.reshape(-1))


def _na_kernel(q_ref, k_ref, v_ref, b_ref, o_ref, *, rows):
    lane = lax.broadcasted_iota(jnp.int32, (GRID_W, LANES), 1)
    first = lane < NA_HEAD_DIM
    n_keys = NA_ROWS * GRID_W

    def one_row(r, carry):
        rs = jnp.clip(r - NA_ROWS // 2, 0, rows - NA_ROWS)
        q = q_ref[0, r]
        zero = jnp.zeros_like(q)
        qs = jnp.concatenate([jnp.where(first, q, zero), jnp.where(first, zero, q)], axis=0)
        kw = k_ref[0, pl.ds(rs, NA_ROWS)].reshape(n_keys, LANES)
        vw = v_ref[0, pl.ds(rs, NA_ROWS)].reshape(n_keys, LANES)
        s = _dot_nt(qs, kw) + b_ref[rs - r + NA_ROWS - 1, 0]
        m = jnp.max(s, axis=-1, keepdims=True)
        e = jnp.exp(s - m)
        den = jnp.sum(e, axis=-1, keepdims=True)
        o = _dot(e.astype(BF16), vw) / den
        o_ref[0, r] = jnp.where(first, o[:GRID_W], o[GRID_W:]).astype(o_ref.dtype)
        return carry

    lax.fori_loop(0, rows, one_row, 0)


def _na(q, k, v, bias, bsz, rows):
    t, width = q.shape
    pairs = width // LANES
    shape4 = (bsz, rows, GRID_W, width)
    blk = pl.BlockSpec((1, rows, GRID_W, LANES), lambda b, i: (b, 0, 0, i))
    out = pl.pallas_call(
        functools.partial(_na_kernel, rows=rows),
        grid=(bsz, pairs),
        in_specs=[blk, blk, blk,
                  pl.BlockSpec((NA_ROWS, 1, 2 * GRID_W, NA_ROWS * GRID_W), lambda b, i: (0, i, 0, 0))],
        out_specs=blk,
        out_shape=jax.ShapeDtypeStruct(shape4, BF16),
        compiler_params=_params(2),
    )(q.reshape(shape4), k.reshape(shape4), v.reshape(shape4), bias)
    return out.reshape(t, width)


def _out_even_kernel(u_ref, ys_ref, na_ref, x_ref, d_ref, wglu_ref, bglu_ref, wo_ref, o_ref):
    half = u_ref.shape[1]
    y = d_ref[...] * u_ref[...] + ys_ref[...]
    g = jax.nn.gelu(y)
    a = g * jax.nn.sigmoid(_dot(g.astype(BF16), wglu_ref[...]) + bglu_ref[...])
    mix = _dot(a.astype(BF16), wo_ref[:half, :]) + _dot(na_ref[...], wo_ref[half:, :])
    o_ref[...] = x_ref[...] + mix


def _out_even(u, ys, na, x, d_skip, w_glu, b_glu, w_out):
    t, d = x.shape
    half = u.shape[1]
    tm = TOKEN_TILE
    row = lambda w: pl.BlockSpec((tm, w), lambda i: (i, 0))
    full = lambda r, w: pl.BlockSpec((r, w), lambda i: (0, 0))
    return pl.pallas_call(
        _out_even_kernel,
        grid=(t // tm,),
        in_specs=[row(half), row(half), row(half), row(d), full(1, half), full(half, half), full(1, half),
                  full(d, d)],
        out_specs=row(d),
        out_shape=jax.ShapeDtypeStruct((t, d), F32),
        compiler_params=_params(1),
    )(u, ys, na, x, d_skip, w_glu, b_glu, w_out)


def _t5_index_table():
    half = T5_BUCKETS // 2
    max_exact = half // 2
    rel = np.arange(3 * BLOCK)[None, :] - BLOCK - np.arange(BLOCK)[:, None]
    n = np.abs(rel)
    assert T5_MAX_DIST // max_exact == 16 and half - max_exact == 8 and max_exact == 8
    sq = np.maximum(n, 1).astype(np.int64) ** 2
    log2_sq = np.floor(np.log2(sq.astype(np.float64) + 0.5)).astype(np.int64)
    large = np.minimum(max_exact + log2_sq - 6, half - 1)
    bucket = np.where(rel > 0, half, 0) + np.where(n < max_exact, n, large)
    return np.where(n <= WINDOW, bucket, -1).astype(np.int32)


def _t5_bias_kernel(idx_ref, t5_ref, o_ref, *, heads):
    h = pl.program_id(0)
    idx = idx_ref[...]
    acc = jnp.full(idx.shape, NEG_INF, F32)
    for b in range(T5_BUCKETS):
        acc = jnp.where(idx == b, t5_ref[b * heads + h], acc)
    o_ref[0] = acc


def _t5_bias(t5_table):
    heads = t5_table.shape[1]
    idx = jnp.asarray(_t5_index_table())
    per_pair = 2 * GQA_GROUP
    return pl.pallas_call(
        functools.partial(_t5_bias_kernel, heads=heads),
        grid=(heads,),
        in_specs=[pl.BlockSpec((BLOCK, 3 * BLOCK), lambda h: (0, 0)),
                  pl.BlockSpec(memory_space=pltpu.SMEM)],
        out_specs=pl.BlockSpec(
            (1, BLOCK, 3 * BLOCK),
            lambda h: ((h // per_pair) * GQA_GROUP + h % GQA_GROUP, (h % per_pair) // GQA_GROUP, 0)),
        out_shape=jax.ShapeDtypeStruct((heads // 2, 2 * BLOCK, 3 * BLOCK), F32),
        compiler_params=_params(1),
    )(idx, t5_table.reshape(-1))


def _gqa_head_order(heads):
    order = []
    for tt in range(heads // 2):
        pair, t = divmod(tt, GQA_GROUP)
        order += [2 * GQA_GROUP * pair + t, 2 * GQA_GROUP * pair + GQA_GROUP + t]
    return order


def _gqa_kernel(q_ref, kp_ref, kc_ref, kn_ref, vp_ref, vc_ref, vn_ref, bias_ref, sink_ref, o_ref, *, nb, heads):
    n = pl.program_id(1)
    k = jnp.concatenate([kp_ref[...], kc_ref[...], kn_ref[...]], axis=0)
    v = jnp.concatenate([vp_ref[...], vc_ref[...], vn_ref[...]], axis=0)
    col = lax.broadcasted_iota(jnp.int32, (1, 3 * BLOCK), 1)
    outside = ((col < BLOCK) & (n == 0)) | ((col >= 2 * BLOCK) & (n == nb - 1))
    edge = jnp.where(outside, NEG_INF, 0.0).astype(F32)
    lane = lax.broadcasted_iota(jnp.int32, (BLOCK, LANES), 1)
    first = lane < GQA_HEAD_DIM
    upper = lax.broadcasted_iota(jnp.int32, (2 * BLOCK, 1), 0) < BLOCK
    order = _gqa_head_order(heads)
    for tt in range(heads // 2):
        pair = tt // GQA_GROUP
        k2 = k[:, pair * LANES:(pair + 1) * LANES]
        v2 = v[:, pair * LANES:(pair + 1) * LANES]
        tile = q_ref[:, tt * LANES:(tt + 1) * LANES]
        zero = jnp.zeros_like(tile)
        qs = jnp.concatenate([jnp.where(first, tile, zero), jnp.where(first, zero, tile)], axis=0)
        s = _dot_nt(qs, k2) + bias_ref[tt] + edge
        sk = jnp.where(upper, sink_ref[order[2 * tt]], sink_ref[order[2 * tt + 1]])
        m = jnp.maximum(jnp.max(s, axis=-1, keepdims=True), sk)
        e = jnp.exp(s - m)
        den = jnp.sum(e, axis=-1, keepdims=True) + jnp.exp(sk - m)
        o = _dot(e.astype(BF16), v2) / den
        o_ref[:, tt * LANES:(tt + 1) * LANES] = jnp.where(first, o[:BLOCK], o[BLOCK:]).astype(o_ref.dtype)


def _gqa(q, k, v, bias, sink, bsz, nb):
    t, qw = q.shape
    kw = k.shape[1]
    heads = qw // GQA_HEAD_DIM
    cur = lambda w: pl.BlockSpec((BLOCK, w), lambda b, n: (b * nb + n, 0))
    prev = lambda w: pl.BlockSpec((BLOCK, w), lambda b, n: (b * nb + jnp.maximum(n - 1, 0), 0))
    nxt = lambda w: pl.BlockSpec((BLOCK, w), lambda b, n: (b * nb + jnp.minimum(n + 1, nb - 1), 0))
    return pl.pallas_call(
        functools.partial(_gqa_kernel, nb=nb, heads=heads),
        grid=(bsz, nb),
        in_specs=[cur(qw), prev(kw), cur(kw), nxt(kw), prev(kw), cur(kw), nxt(kw),
                  pl.BlockSpec((heads // 2, 2 * BLOCK, 3 * BLOCK), lambda b, n: (0, 0, 0)),
                  pl.BlockSpec(memory_space=pltpu.SMEM)],
        out_specs=cur(qw),
        out_shape=jax.ShapeDtypeStruct((t, qw), BF16),
        compiler_params=_params(2),
    )(q, k, k, k, v, v, v, bias, sink)


def _out_odd_kernel(a_ref, x_ref, wo_ref, o_ref):
    o_ref[...] = x_ref[...] + _dot(a_ref[...], wo_ref[...])


def _out_odd(a, x, w_out):
    t, d = x.shape
    tm = TOKEN_TILE
    return pl.pallas_call(
        _out_odd_kernel,
        grid=(t // tm,),
        in_specs=[pl.BlockSpec((tm, a.shape[1]), lambda i: (i, 0)),
                  pl.BlockSpec((tm, d), lambda i: (i, 0)),
                  pl.BlockSpec(w_out.shape, lambda i: (0, 0))],
        out_specs=pl.BlockSpec((tm, d), lambda i: (i, 0)),
        out_shape=jax.ShapeDtypeStruct((t, d), F32),
        compiler_params=_params(1),
    )(a, x, w_out)


def _to_chunk_major(u, bsz, nc):
    groups = u.shape[1] // S5_GROUP
    u5 = u.astype(BF16).reshape(bsz, nc, CHUNK, groups, S5_GROUP)
    return jnp.transpose(u5, (3, 1, 0, 2, 4)).reshape(groups, nc * bsz, CHUNK_W)


def _to_token_major(y, bsz, nc):
    groups = y.shape[0]
    y5 = y.reshape(groups, nc, bsz, CHUNK, S5_GROUP)
    return jnp.transpose(y5, (2, 1, 3, 0, 4)).reshape(bsz * nc * CHUNK, groups * S5_GROUP)


def _prepare(p):
    depth = p["norm_mix"].shape[0]
    row = lambda a: a.reshape(1, -1).astype(F32)
    prep = {"depth": depth, "layers": []}
    heads = p["gqa_sink"].shape[-1] if depth > 1 else 0
    if depth > 1:
        order = np.asarray(_gqa_head_order(heads))
        slot_cols = (order[:, None] * GQA_HEAD_DIM + np.arange(GQA_HEAD_DIM)[None, :]).reshape(-1)
        prep["t5_bias"] = _t5_bias(p["t5_table"].astype(F32))
    for layer in range(depth):
        i = layer // 2
        lp = {
            "ffn": [(row(p["norm_ffn"][layer, j]), p["w_ffn_gate"][layer, j].astype(BF16),
                     p["w_ffn_up"][layer, j].astype(BF16), p["w_ffn_down"][layer, j].astype(BF16))
                    for j in range(2)],
            "norm_mix": row(p["norm_mix"][layer]),
        }
        if layer % 2 == 0:
            lp["w_in"] = p["w_in_even"][i].astype(BF16)
            lp["tables"] = _s5_tables(p["s5_lam_re"][i].astype(F32), p["s5_lam_im"][i].astype(F32),
                                      p["s5_log_dt"][i].astype(F32), p["s5_b_re"][i].astype(F32),
                                      p["s5_b_im"][i].astype(F32), p["s5_c_re"][i].astype(F32),
                                      p["s5_c_im"][i].astype(F32))
            lp["d_skip"] = row(p["s5_d"][i])
            lp["w_glu"] = p["s5_w_glu"][i].astype(BF16)
            lp["b_glu"] = row(p["s5_b_glu"][i])
            lp["na_bias"] = _na_bias(p["na_rpb"][i].astype(F32))
            lp["w_out"] = p["w_out_even"][i].astype(BF16)
        else:
            w_in = p["w_in_odd"][i]
            nq = heads * GQA_HEAD_DIM
            lp["w_in"] = jnp.concatenate([w_in[:, slot_cols], w_in[:, nq:]], axis=1).astype(BF16)
            lp["sink"] = p["gqa_sink"][i].astype(F32)
            lp["w_out"] = p["w_out_odd"][i][slot_cols, :].astype(BF16)
        prep["layers"].append(lp)
    prep["norm_final"] = row(p["norm_final"])
    return prep


def _trunk(x, prep):
    bsz, seq, d = x.shape
    t = bsz * seq
    x = x.reshape(t, d).astype(F32)
    depth = prep["depth"]
    for layer, lp in enumerate(prep["layers"]):
        x = _ffn(x, *lp["ffn"][0], prep["norm_final"], False)
        if layer % 2 == 0:
            half = d // 2
            scale = NA_HEAD_DIM ** -0.5
            u, q, k, v = _inproj(x, lp["norm_mix"], lp["w_in"],
                                 [(half, F32, 1.0), (half, BF16, scale), (half, BF16, 1.0), (half, BF16, 1.0)])
            nc = seq // CHUNK
            ys = _to_token_major(_s5_apply(_to_chunk_major(u, bsz, nc), *lp["tables"], nc, bsz), bsz, nc)
            na = _na(q, k, v, lp["na_bias"], bsz, seq // GRID_W)
            x = _out_even(u, ys, na, x, lp["d_skip"], lp["w_glu"], lp["b_glu"], lp["w_out"])
        else:
            heads = lp["sink"].shape[0]
            nq = heads * GQA_HEAD_DIM
            nkv = nq // GQA_GROUP
            scale = GQA_HEAD_DIM ** -0.5
            q, k, v = _inproj(x, lp["norm_mix"], lp["w_in"],
                              [(nq, BF16, scale), (nkv, BF16, 1.0), (nkv, BF16, 1.0)])
            a = _gqa(q, k, v, prep["t5_bias"], lp["sink"], bsz, seq // BLOCK)
            x = _out_odd(a, x, lp["w_out"])
        x = _ffn(x, *lp["ffn"][1], prep["norm_final"], layer == depth - 1)
    return x.reshape(bsz, seq, d)


def kernel(x_prompt, x_sample, norm_ffn, w_ffn_gate, w_ffn_up, w_ffn_down, norm_mix, w_in_even, s5_lam_re, s5_lam_im, s5_log_dt, s5_b_re, s5_b_im, s5_c_re, s5_c_im, s5_d, s5_w_glu, s5_b_glu, na_rpb, w_out_even, w_in_odd, gqa_sink, w_out_odd, t5_table, norm_final):
    prep = _prepare(dict(
        norm_ffn=norm_ffn, w_ffn_gate=w_ffn_gate, w_ffn_up=w_ffn_up, w_ffn_down=w_ffn_down,
        norm_mix=norm_mix, w_in_even=w_in_even, s5_lam_re=s5_lam_re, s5_lam_im=s5_lam_im,
        s5_log_dt=s5_log_dt, s5_b_re=s5_b_re, s5_b_im=s5_b_im, s5_c_re=s5_c_re, s5_c_im=s5_c_im,
        s5_d=s5_d, s5_w_glu=s5_w_glu, s5_b_glu=s5_b_glu, na_rpb=na_rpb, w_out_even=w_out_even,
        w_in_odd=w_in_odd, gqa_sink=gqa_sink, w_out_odd=w_out_odd, t5_table=t5_table,
        norm_final=norm_final))
    return (_trunk(x_prompt, prep), _trunk(x_sample, prep))
```

```python
import functools
import math

import jax
import jax.numpy as jnp
import numpy as np
from jax import lax
from jax.experimental import pallas as pl
from jax.experimental.pallas import tpu as pltpu

F32 = jnp.float32
BF16 = jnp.bfloat16

S5_GROUP = 16
S5_STATE = 64
NA_HEAD_DIM = 64
NA_ROWS = 8
NA_COLS = 16
GRID_W = 64
GQA_HEAD_DIM = 64
GQA_GROUP = 4
WINDOW = 128
BLOCK = 128
T5_BUCKETS = 32
T5_MAX_DIST = 128
RMS_EPS = 1e-6
NEG_INF = -1e30

CHUNK = 64
CHUNK_W = CHUNK * S5_GROUP
LANES = 128
TOKEN_TILE = 512
FFN_TOKEN_TILE = 1024
FFN_CHUNK = 512
NA_ROWS_PER_STEP = 8
VMEM_LIMIT = 48 * 1024 * 1024
FFN_VMEM_LIMIT = 56 * 1024 * 1024

_NT = (((1,), (1,)), ((), ()))
_HI = lax.Precision.HIGHEST


def _params(n_axes):
    return pltpu.CompilerParams(dimension_semantics=("arbitrary",) * n_axes,
                                vmem_limit_bytes=VMEM_LIMIT)


def _rms(x, g):
    return x * lax.rsqrt(jnp.mean(x * x, axis=-1, keepdims=True) + RMS_EPS) * g


def _dot(a, b):
    return jnp.dot(a, b, preferred_element_type=F32)


def _dot_nt(a, b, precision=None):
    return lax.dot_general(a, b, _NT, precision=precision, preferred_element_type=F32)


def _ffn_kernel(x_ref, g_ref, wg_ref, wu_ref, wd_ref, gf_ref, o_ref, *, final_norm, bounds):
    x = x_ref[...]
    hn = _rms(x, g_ref[...]).astype(BF16)
    acc = None
    for lo, hi in bounds:
        gate = _dot(hn, wg_ref[:, lo:hi])
        up = _dot(hn, wu_ref[:, lo:hi])
        h = (gate * jax.nn.sigmoid(gate) * up).astype(BF16)
        part = _dot(h, wd_ref[lo:hi, :])
        acc = part if acc is None else acc + part
    y = x + 0.5 * acc
    if final_norm:
        y = _rms(y, gf_ref[...])
    o_ref[...] = y


def _ffn(x, g, wg, wu, wd, g_final, final_norm):
    t, d = x.shape
    f = wg.shape[1]
    tm = FFN_TOKEN_TILE
    bounds = tuple((lo, min(lo + FFN_CHUNK, f)) for lo in range(0, f, FFN_CHUNK))
    resident = lambda shape: pl.BlockSpec(shape, lambda i: (0, 0), pipeline_mode=pl.Buffered(1))
    return pl.pallas_call(
        functools.partial(_ffn_kernel, final_norm=final_norm, bounds=bounds),
        grid=(t // tm,),
        in_specs=[
            pl.BlockSpec((tm, d), lambda i: (i, 0)),
            pl.BlockSpec((1, d), lambda i: (0, 0)),
            resident((d, f)),
            resident((d, f)),
            resident((f, d)),
            pl.BlockSpec((1, d), lambda i: (0, 0)),
        ],
        out_specs=pl.BlockSpec((tm, d), lambda i: (i, 0)),
        out_shape=jax.ShapeDtypeStruct((t, d), F32),
        compiler_params=pltpu.CompilerParams(dimension_semantics=("arbitrary",),
                                             vmem_limit_bytes=FFN_VMEM_LIMIT),
        name="ffn",
    )(x, g, wg, wu, wd, g_final)


def _inproj_kernel(x_ref, g_ref, w_ref, *o_refs, scales):
    hn = _rms(x_ref[...], g_ref[...]).astype(BF16)
    off = 0
    for o_ref, scale in zip(o_refs, scales):
        width = o_ref.shape[1]
        part = _dot(hn, w_ref[:, off:off + width])
        if scale != 1.0:
            part = part * scale
        o_ref[...] = part.astype(o_ref.dtype)
        off += width


def _inproj(x, g, w, splits):
    t, d = x.shape
    n = w.shape[1]
    tm = TOKEN_TILE
    return pl.pallas_call(
        functools.partial(_inproj_kernel, scales=tuple(s for _, _, s in splits)),
        grid=(t // tm,),
        in_specs=[
            pl.BlockSpec((tm, d), lambda i: (i, 0)),
            pl.BlockSpec((1, d), lambda i: (0, 0)),
            pl.BlockSpec((d, n), lambda i: (0, 0)),
        ],
        out_specs=[pl.BlockSpec((tm, width), lambda i: (i, 0)) for width, _, _ in splits],
        out_shape=[jax.ShapeDtypeStruct((t, width), dt) for width, dt, _ in splits],
        compiler_params=_params(1),
        name="inproj",
    )(x, g, w)


def _cmul(ar, ai, br, bi):
    return ar * br - ai * bi, ar * bi + ai * br


def _s5_table_kernel(lre_ref, lim_ref, ldt_ref, btr_ref, bti_ref, cr_ref, ci_ref,
                     m_ref, zb_ref, oc_ref, aq_ref):
    q = CHUNK
    p = S5_STATE
    c = S5_GROUP
    two_q = 2 * q
    eye = (lax.broadcasted_iota(jnp.int32, (p, p), 0) == lax.broadcasted_iota(jnp.int32, (p, p), 1)).astype(F32)

    def expand(n_of_row, table):
        rows = n_of_row.shape[0]
        sel = (n_of_row == lax.broadcasted_iota(jnp.int32, (rows, two_q), 1)).astype(F32)
        return jnp.dot(sel, table, precision=_HI, preferred_element_type=F32)

    def tile_rows(a, reps):
        return jnp.concatenate([a] * reps, axis=0)

    row2 = lax.broadcasted_iota(jnp.int32, (two_q * c, 1), 0) // c
    row1 = lax.broadcasted_iota(jnp.int32, (q * c, 1), 0) // c

    kernel_t = None
    for d in range(2):
        lr = lre_ref[d, 0]
        li = lim_ref[d, 0]
        dt = jnp.exp(ldt_ref[d, 0])
        mag = jnp.exp(lr * dt)
        ar = mag * jnp.cos(li * dt)
        ai = mag * jnp.sin(li * dt)
        den = lr * lr + li * li
        nr = ar - 1.0
        zr = (nr * lr + ai * li) / den
        zi = (ai * lr - nr * li) / den
        btr = btr_ref[d, 0]
        bti = bti_ref[d, 0]
        bbr = zr * btr - zi * bti
        bbi = zr * bti + zi * btr
        cre = cr_ref[d, 0]
        cim = ci_ref[d, 0]

        n_col = lax.broadcasted_iota(jnp.int32, (two_q, 1), 0)
        pr = jnp.ones((two_q, p), F32)
        pi = jnp.zeros((two_q, p), F32)
        sr, si = ar, ai
        for j in range(two_q.bit_length() - 1):
            bit = ((n_col >> j) & 1) == 1
            mr, mi = _cmul(pr, pi, sr, si)
            pr = jnp.where(bit, mr, pr)
            pi = jnp.where(bit, mi, pi)
            sr, si = _cmul(sr, si, sr, si)
        aq_re = pr[q:q + 1]
        aq_im = pi[q:q + 1]

        lag = (row2 - q) if d == 0 else (q - row2)
        er, ei = expand(lag, pr), expand(lag, pi)
        car, cai = _cmul(tile_rows(cre, two_q), tile_rows(cim, two_q), er, ei)
        kt = _dot_nt(bbr, car, _HI) - _dot_nt(bbi, cai, _HI)
        kernel_t = kt if kernel_t is None else kernel_t + kt

        n_z = (q - 1 - row1) if d == 0 else row1
        zbr, zbi = _cmul(tile_rows(bbr, q), tile_rows(bbi, q), expand(n_z, pr), expand(n_z, pi))
        zb_ref[0, :, d * p:(d + 1) * p] = zbr.astype(BF16)
        zb_ref[0, :, 2 * p + d * p:2 * p + (d + 1) * p] = zbi.astype(BF16)

        n_o = (row1 + 1) if d == 0 else (q - row1)
        e_re, e_im = _cmul(tile_rows(cre, q), tile_rows(cim, q), expand(n_o, pr), expand(n_o, pi))
        oc_ref[0, d * p:(d + 1) * p, :] = _dot_nt(eye, e_re, _HI).astype(BF16)
        oc_ref[0, 2 * p + d * p:2 * p + (d + 1) * p, :] = (-_dot_nt(eye, e_im, _HI)).astype(BF16)

        aq_ref[0, :, d * p:(d + 1) * p] = aq_re
        aq_ref[0, :, 2 * p + d * p:2 * p + (d + 1) * p] = aq_im

    per_vreg = LANES // c
    width = two_q * c
    for rho in range(per_vreg):
        rolled = kernel_t if rho == 0 else pltpu.roll(kernel_t, width - c * rho, axis=1)
        for m in range(q // per_vreg):
            s = (per_vreg - rho) % per_vreg + per_vreg * m
            start = (q - s - rho) // per_vreg * LANES
            m_ref[0, s * c:(s + 1) * c, :] = rolled[:, start:start + q * c].astype(BF16)


def _s5_tables(lam_re, lam_im, log_dt, b_re, b_im, c_re, c_im):
    g = lam_re.shape[1]
    p, c, q = S5_STATE, S5_GROUP, CHUNK
    vec = lambda a: a.reshape(2, g, 1, a.shape[-1] if a.ndim == 3 else 1)
    swap = lambda a: jnp.swapaxes(a, -1, -2)
    spec4 = lambda r, w: pl.BlockSpec((2, 1, r, w), lambda i: (0, i, 0, 0))
    return pl.pallas_call(
        _s5_table_kernel,
        grid=(g,),
        in_specs=[spec4(1, p), spec4(1, p), spec4(1, 1), spec4(c, p), spec4(c, p), spec4(c, p), spec4(c, p)],
        out_specs=[
            pl.BlockSpec((1, q * c, q * c), lambda i: (i, 0, 0)),
            pl.BlockSpec((1, q * c, 4 * p), lambda i: (i, 0, 0)),
            pl.BlockSpec((1, 4 * p, q * c), lambda i: (i, 0, 0)),
            pl.BlockSpec((1, 1, 4 * p), lambda i: (i, 0, 0)),
        ],
        out_shape=[
            jax.ShapeDtypeStruct((g, q * c, q * c), BF16),
            jax.ShapeDtypeStruct((g, q * c, 4 * p), BF16),
            jax.ShapeDtypeStruct((g, 4 * p, q * c), BF16),
            jax.ShapeDtypeStruct((g, 1, 4 * p), F32),
        ],
        compiler_params=_params(1),
        name="s5_tables",
    )(vec(lam_re), vec(lam_im), vec(log_dt), swap(b_re), swap(b_im), c_re, c_im)


def _s5_kernel(u_ref, m_ref, zb_ref, oc_ref, aq_ref, y_ref, z_scr, sf_scr, sb_scr, *, nc, bsz):
    half = 2 * S5_STATE
    u = u_ref[0]
    z_scr[...] = _dot(u, zb_ref[0])
    aq = aq_ref[0]
    ar, ai = aq[:, :half], aq[:, half:]
    zero = jnp.zeros((bsz, half), F32)

    def step(j, carry):
        fr, fi, br, bi = carry
        rf = pl.multiple_of(j * bsz, 8)
        rb = pl.multiple_of((nc - 1 - j) * bsz, 8)
        sf_scr[pl.ds(rf, bsz), :half] = fr
        sf_scr[pl.ds(rf, bsz), half:] = fi
        sb_scr[pl.ds(rb, bsz), :half] = br
        sb_scr[pl.ds(rb, bsz), half:] = bi
        zf = z_scr[pl.ds(rf, bsz), :]
        zb = z_scr[pl.ds(rb, bsz), :]
        nfr, nfi = _cmul(ar, ai, fr, fi)
        nbr, nbi = _cmul(ar, ai, br, bi)
        return nfr + zf[:, :half], nfi + zf[:, half:], nbr + zb[:, :half], nbi + zb[:, half:]

    lax.fori_loop(0, nc, step, (zero, zero, zero, zero))
    lane = lax.broadcasted_iota(jnp.int32, (1, 2 * half), 1)
    is_fwd = (lane & S5_STATE) == 0
    s = jnp.where(is_fwd, sf_scr[...], sb_scr[...]).astype(BF16)
    y_ref[0] = (_dot(u, m_ref[0]) + _dot(s, oc_ref[0])).astype(y_ref.dtype)


def _s5_apply(u_cm, m, zb, oc, aq, nc, bsz):
    g, rows, w = u_cm.shape
    st = zb.shape[2]
    return pl.pallas_call(
        functools.partial(_s5_kernel, nc=nc, bsz=bsz),
        grid=(g,),
        in_specs=[
            pl.BlockSpec((1, rows, w), lambda i: (i, 0, 0)),
            pl.BlockSpec((1, w, w), lambda i: (i, 0, 0)),
            pl.BlockSpec((1, w, st), lambda i: (i, 0, 0)),
            pl.BlockSpec((1, st, w), lambda i: (i, 0, 0)),
            pl.BlockSpec((1, 1, st), lambda i: (i, 0, 0)),
        ],
        out_specs=pl.BlockSpec((1, rows, w), lambda i: (i, 0, 0)),
        out_shape=jax.ShapeDtypeStruct((g, rows, w), BF16),
        scratch_shapes=[pltpu.VMEM((rows, st), F32)] * 3,
        compiler_params=_params(1),
        name="s5_apply",
    )(u_cm, m, zb, oc, aq)


def _na_bias_kernel(rpb_ref, o_ref):
    st = pl.program_id(0)
    h = pl.program_id(1)
    n_rel_rows = 2 * NA_ROWS - 1
    n_rel_cols = 2 * NA_COLS - 1
    w = lax.broadcasted_iota(jnp.int32, (GRID_W, LANES), 0)
    lane = lax.broadcasted_iota(jnp.int32, (GRID_W, LANES), 1)
    wk = lane & (GRID_W - 1)
    first = lane < GRID_W
    cs = jnp.clip(w - NA_COLS // 2, 0, GRID_W - NA_COLS)
    valid = (wk >= cs) & (wk < cs + NA_COLS)
    rel = wk - w + NA_COLS - 1
    for i in range(NA_ROWS * GRID_W // LANES):
        base0 = (h * n_rel_rows + st + 2 * i) * n_rel_cols
        base1 = base0 + n_rel_cols
        acc = jnp.full((GRID_W, LANES), NEG_INF, F32)
        for j in range(n_rel_cols):
            val = jnp.where(first, rpb_ref[base0 + j], rpb_ref[base1 + j])
            acc = jnp.where(valid & (rel == j), val, acc)
        o_ref[0, 0, :, i * LANES:(i + 1) * LANES] = acc


def _na_bias(rpb):
    h = rpb.shape[0]
    pairs = h // 2
    return pl.pallas_call(
        _na_bias_kernel,
        grid=(NA_ROWS, h),
        in_specs=[pl.BlockSpec(memory_space=pltpu.SMEM)],
        out_specs=pl.BlockSpec((1, 1, GRID_W, NA_ROWS * GRID_W), lambda s, i: (s, i // 2, i % 2, 0)),
        out_shape=jax.ShapeDtypeStruct((NA_ROWS, pairs, 2 * GRID_W, NA_ROWS * GRID_W), F32),
        compiler_params=_params(2),
        name="na_bias",
    )(rpb.reshape(-1))


def _na_kernel(q_ref, k_ref, v_ref, b_ref, o_ref, s_scr, p_scr, *, rows):
    lane = lax.broadcasted_iota(jnp.int32, (GRID_W, LANES), 1)
    first = lane < NA_HEAD_DIM
    n_keys = NA_ROWS * GRID_W

    def row_start(r):
        return jnp.clip(r - NA_ROWS // 2, 0, rows - NA_ROWS)

    def row_group(i, carry):
        base = i * NA_ROWS_PER_STEP
        for j in range(NA_ROWS_PER_STEP):
            r = base + j
            rs = row_start(r)
            q = q_ref[0, r]
            zero = jnp.zeros_like(q)
            qs = jnp.concatenate([jnp.where(first, q, zero), jnp.where(first, zero, q)], axis=0)
            kw = k_ref[0, pl.ds(rs, NA_ROWS)].reshape(n_keys, LANES)
            s_scr[j] = _dot_nt(qs, kw) + b_ref[rs - r + NA_ROWS - 1, 0]
        s = s_scr[...]
        e = jnp.exp(s - jnp.max(s, axis=-1, keepdims=True))
        den = jnp.sum(e, axis=-1, keepdims=True)
        p_scr[...] = e.astype(BF16)
        for j in range(NA_ROWS_PER_STEP):
            r = base + j
            vw = v_ref[0, pl.ds(row_start(r), NA_ROWS)].reshape(n_keys, LANES)
            o = _dot(p_scr[j], vw) / den[j]
            o_ref[0, r] = jnp.where(first, o[:GRID_W], o[GRID_W:]).astype(o_ref.dtype)
        return carry

    lax.fori_loop(0, rows // NA_ROWS_PER_STEP, row_group, 0)


def _na(q, k, v, bias, bsz, rows):
    t, width = q.shape
    pairs = width // LANES
    shape4 = (bsz, rows, GRID_W, width)
    blk = pl.BlockSpec((1, rows, GRID_W, LANES), lambda b, i: (b, 0, 0, i))
    out = pl.pallas_call(
        functools.partial(_na_kernel, rows=rows),
        grid=(bsz, pairs),
        in_specs=[blk, blk, blk,
                  pl.BlockSpec((NA_ROWS, 1, 2 * GRID_W, NA_ROWS * GRID_W), lambda b, i: (0, i, 0, 0))],
        out_specs=blk,
        out_shape=jax.ShapeDtypeStruct(shape4, BF16),
        scratch_shapes=[pltpu.VMEM((NA_ROWS_PER_STEP, 2 * GRID_W, NA_ROWS * GRID_W), F32),
                        pltpu.VMEM((NA_ROWS_PER_STEP, 2 * GRID_W, NA_ROWS * GRID_W), BF16)],
        compiler_params=_params(2),
        name="na_attn",
    )(q.reshape(shape4), k.reshape(shape4), v.reshape(shape4), bias)
    return out.reshape(t, width)


def _out_even_kernel(u_ref, ys_ref, na_ref, x_ref, d_ref, wglu_ref, bglu_ref, wo_ref, o_ref):
    half = u_ref.shape[1]
    y = d_ref[...] * u_ref[...] + ys_ref[...]
    g = jax.nn.gelu(y)
    a = g * jax.nn.sigmoid(_dot(g.astype(BF16), wglu_ref[...]) + bglu_ref[...])
    mix = _dot(a.astype(BF16), wo_ref[:half, :]) + _dot(na_ref[...], wo_ref[half:, :])
    o_ref[...] = x_ref[...] + mix


def _out_even(u, ys, na, x, d_skip, w_glu, b_glu, w_out):
    t, d = x.shape
    half = u.shape[1]
    tm = TOKEN_TILE
    row = lambda w: pl.BlockSpec((tm, w), lambda i: (i, 0))
    full = lambda r, w: pl.BlockSpec((r, w), lambda i: (0, 0))
    return pl.pallas_call(
        _out_even_kernel,
        grid=(t // tm,),
        in_specs=[row(half), row(half), row(half), row(d), full(1, half), full(half, half), full(1, half),
                  full(d, d)],
        out_specs=row(d),
        out_shape=jax.ShapeDtypeStruct((t, d), F32),
        compiler_params=_params(1),
        name="out_even",
    )(u, ys, na, x, d_skip, w_glu, b_glu, w_out)


def _t5_index_table():
    half = T5_BUCKETS // 2
    max_exact = half // 2
    rel = np.arange(3 * BLOCK)[None, :] - BLOCK - np.arange(BLOCK)[:, None]
    n = np.abs(rel)
    assert T5_MAX_DIST // max_exact == 16 and half - max_exact == 8 and max_exact == 8
    sq = np.maximum(n, 1).astype(np.int64) ** 2
    log2_sq = np.floor(np.log2(sq.astype(np.float64) + 0.5)).astype(np.int64)
    large = np.minimum(max_exact + log2_sq - 6, half - 1)
    bucket = np.where(rel > 0, half, 0) + np.where(n < max_exact, n, large)
    return np.where(n <= WINDOW, bucket, -1).astype(np.int32)


def _t5_bias_kernel(idx_ref, t5_ref, o_ref, *, heads):
    h = pl.program_id(0)
    idx = idx_ref[...]
    acc = jnp.full(idx.shape, NEG_INF, F32)
    for b in range(T5_BUCKETS):
        acc = jnp.where(idx == b, t5_ref[b * heads + h], acc)
    o_ref[0] = acc


def _t5_bias(t5_table):
    heads = t5_table.shape[1]
    idx = jnp.asarray(_t5_index_table())
    per_pair = 2 * GQA_GROUP
    return pl.pallas_call(
        functools.partial(_t5_bias_kernel, heads=heads),
        grid=(heads,),
        in_specs=[pl.BlockSpec((BLOCK, 3 * BLOCK), lambda h: (0, 0)),
                  pl.BlockSpec(memory_space=pltpu.SMEM)],
        out_specs=pl.BlockSpec(
            (1, BLOCK, 3 * BLOCK),
            lambda h: ((h // per_pair) * GQA_GROUP + h % GQA_GROUP, (h % per_pair) // GQA_GROUP, 0)),
        out_shape=jax.ShapeDtypeStruct((heads // 2, 2 * BLOCK, 3 * BLOCK), F32),
        compiler_params=_params(1),
        name="t5_bias",
    )(idx, t5_table.reshape(-1))


def _gqa_head_order(heads):
    order = []
    for tt in range(heads // 2):
        pair, t = divmod(tt, GQA_GROUP)
        order += [2 * GQA_GROUP * pair + t, 2 * GQA_GROUP * pair + GQA_GROUP + t]
    return order


def _gqa_kernel(q_ref, kp_ref, kc_ref, kn_ref, vp_ref, vc_ref, vn_ref, bias_ref, sink_ref, o_ref, *, nb, heads):
    n = pl.program_id(1)
    k = jnp.concatenate([kp_ref[...], kc_ref[...], kn_ref[...]], axis=0)
    v = jnp.concatenate([vp_ref[...], vc_ref[...], vn_ref[...]], axis=0)
    col = lax.broadcasted_iota(jnp.int32, (1, 3 * BLOCK), 1)
    outside = ((col < BLOCK) & (n == 0)) | ((col >= 2 * BLOCK) & (n == nb - 1))
    edge = jnp.where(outside, NEG_INF, 0.0).astype(F32)
    lane = lax.broadcasted_iota(jnp.int32, (BLOCK, LANES), 1)
    first = lane < GQA_HEAD_DIM
    upper = lax.broadcasted_iota(jnp.int32, (2 * BLOCK, 1), 0) < BLOCK
    order = _gqa_head_order(heads)
    for tt in range(heads // 2):
        pair = tt // GQA_GROUP
        k2 = k[:, pair * LANES:(pair + 1) * LANES]
        v2 = v[:, pair * LANES:(pair + 1) * LANES]
        tile = q_ref[:, tt * LANES:(tt + 1) * LANES]
        zero = jnp.zeros_like(tile)
        qs = jnp.concatenate([jnp.where(first, tile, zero), jnp.where(first, zero, tile)], axis=0)
        s = _dot_nt(qs, k2) + bias_ref[tt] + edge
        sk = jnp.where(upper, sink_ref[order[2 * tt]], sink_ref[order[2 * tt + 1]])
        m = jnp.maximum(jnp.max(s, axis=-1, keepdims=True), sk)
        e = jnp.exp(s - m)
        den = jnp.sum(e, axis=-1, keepdims=True) + jnp.exp(sk - m)
        o = _dot(e.astype(BF16), v2) / den
        o_ref[:, tt * LANES:(tt + 1) * LANES] = jnp.where(first, o[:BLOCK], o[BLOCK:]).astype(o_ref.dtype)


def _gqa(q, k, v, bias, sink, bsz, nb):
    t, qw = q.shape
    kw = k.shape[1]
    heads = qw // GQA_HEAD_DIM
    cur = lambda w: pl.BlockSpec((BLOCK, w), lambda b, n: (b * nb + n, 0))
    prev = lambda w: pl.BlockSpec((BLOCK, w), lambda b, n: (b * nb + jnp.maximum(n - 1, 0), 0))
    nxt = lambda w: pl.BlockSpec((BLOCK, w), lambda b, n: (b * nb + jnp.minimum(n + 1, nb - 1), 0))
    return pl.pallas_call(
        functools.partial(_gqa_kernel, nb=nb, heads=heads),
        grid=(bsz, nb),
        in_specs=[cur(qw), prev(kw), cur(kw), nxt(kw), prev(kw), cur(kw), nxt(kw),
                  pl.BlockSpec((heads // 2, 2 * BLOCK, 3 * BLOCK), lambda b, n: (0, 0, 0)),
                  pl.BlockSpec(memory_space=pltpu.SMEM)],
        out_specs=cur(qw),
        out_shape=jax.ShapeDtypeStruct((t, qw), BF16),
        compiler_params=_params(2),
        name="gqa_attn",
    )(q, k, k, k, v, v, v, bias, sink)


def _out_odd_kernel(a_ref, x_ref, wo_ref, o_ref):
    o_ref[...] = x_ref[...] + _dot(a_ref[...], wo_ref[...])


def _out_odd(a, x, w_out):
    t, d = x.shape
    tm = TOKEN_TILE
    return pl.pallas_call(
        _out_odd_kernel,
        grid=(t // tm,),
        in_specs=[pl.BlockSpec((tm, a.shape[1]), lambda i: (i, 0)),
                  pl.BlockSpec((tm, d), lambda i: (i, 0)),
                  pl.BlockSpec(w_out.shape, lambda i: (0, 0))],
        out_specs=pl.BlockSpec((tm, d), lambda i: (i, 0)),
        out_shape=jax.ShapeDtypeStruct((t, d), F32),
        compiler_params=_params(1),
        name="out_odd",
    )(a, x, w_out)


def _to_chunk_major(u, bsz, nc):
    groups = u.shape[1] // S5_GROUP
    u5 = u.astype(BF16).reshape(bsz, nc, CHUNK, groups, S5_GROUP)
    return jnp.transpose(u5, (3, 1, 0, 2, 4)).reshape(groups, nc * bsz, CHUNK_W)


def _to_token_major(y, bsz, nc):
    groups = y.shape[0]
    y5 = y.reshape(groups, nc, bsz, CHUNK, S5_GROUP)
    return jnp.transpose(y5, (2, 1, 3, 0, 4)).reshape(bsz * nc * CHUNK, groups * S5_GROUP)


def _prepare(p):
    depth = p["norm_mix"].shape[0]
    row = lambda a: a.reshape(1, -1).astype(F32)
    prep = {"depth": depth, "layers": []}
    heads = p["gqa_sink"].shape[-1] if depth > 1 else 0
    if depth > 1:
        order = np.asarray(_gqa_head_order(heads))
        slot_cols = (order[:, None] * GQA_HEAD_DIM + np.arange(GQA_HEAD_DIM)[None, :]).reshape(-1)
        prep["t5_bias"] = _t5_bias(p["t5_table"].astype(F32))
    for layer in range(depth):
        i = layer // 2
        lp = {
            "ffn": [(row(p["norm_ffn"][layer, j]), p["w_ffn_gate"][layer, j].astype(BF16),
                     p["w_ffn_up"][layer, j].astype(BF16), p["w_ffn_down"][layer, j].astype(BF16))
                    for j in range(2)],
            "norm_mix": row(p["norm_mix"][layer]),
        }
        if layer % 2 == 0:
            lp["w_in"] = p["w_in_even"][i].astype(BF16)
            lp["tables"] = _s5_tables(p["s5_lam_re"][i].astype(F32), p["s5_lam_im"][i].astype(F32),
                                      p["s5_log_dt"][i].astype(F32), p["s5_b_re"][i].astype(F32),
                                      p["s5_b_im"][i].astype(F32), p["s5_c_re"][i].astype(F32),
                                      p["s5_c_im"][i].astype(F32))
            lp["d_skip"] = row(p["s5_d"][i])
            lp["w_glu"] = p["s5_w_glu"][i].astype(BF16)
            lp["b_glu"] = row(p["s5_b_glu"][i])
            lp["na_bias"] = _na_bias(p["na_rpb"][i].astype(F32))
            lp["w_out"] = p["w_out_even"][i].astype(BF16)
        else:
            w_in = p["w_in_odd"][i]
            nq = heads * GQA_HEAD_DIM
            lp["w_in"] = jnp.concatenate([w_in[:, slot_cols], w_in[:, nq:]], axis=1).astype(BF16)
            lp["sink"] = p["gqa_sink"][i].astype(F32)
            lp["w_out"] = p["w_out_odd"][i][slot_cols, :].astype(BF16)
        prep["layers"].append(lp)
    prep["norm_final"] = row(p["norm_final"])
    return prep


def _trunk(x, prep):
    bsz, seq, d = x.shape
    t = bsz * seq
    x = x.reshape(t, d).astype(F32)
    depth = prep["depth"]
    for layer, lp in enumerate(prep["layers"]):
        x = _ffn(x, *lp["ffn"][0], prep["norm_final"], False)
        if layer % 2 == 0:
            half = d // 2
            scale = NA_HEAD_DIM ** -0.5
            u, q, k, v = _inproj(x, lp["norm_mix"], lp["w_in"],
                                 [(half, F32, 1.0), (half, BF16, scale), (half, BF16, 1.0), (half, BF16, 1.0)])
            nc = seq // CHUNK
            ys = _to_token_major(_s5_apply(_to_chunk_major(u, bsz, nc), *lp["tables"], nc, bsz), bsz, nc)
            na = _na(q, k, v, lp["na_bias"], bsz, seq // GRID_W)
            x = _out_even(u, ys, na, x, lp["d_skip"], lp["w_glu"], lp["b_glu"], lp["w_out"])
        else:
            heads = lp["sink"].shape[0]
            nq = heads * GQA_HEAD_DIM
            nkv = nq // GQA_GROUP
            scale = GQA_HEAD_DIM ** -0.5
            q, k, v = _inproj(x, lp["norm_mix"], lp["w_in"],
                              [(nq, BF16, scale), (nkv, BF16, 1.0), (nkv, BF16, 1.0)])
            a = _gqa(q, k, v, prep["t5_bias"], lp["sink"], bsz, seq // BLOCK)
            x = _out_odd(a, x, lp["w_out"])
        x = _ffn(x, *lp["ffn"][1], prep["norm_final"], layer == depth - 1)
    return x.reshape(bsz, seq, d)


def kernel(x_prompt, x_sample, norm_ffn, w_ffn_gate, w_ffn_up, w_ffn_down, norm_mix, w_in_even, s5_lam_re, s5_lam_im, s5_log_dt, s5_b_re, s5_b_im, s5_c_re, s5_c_im, s5_d, s5_w_glu, s5_b_glu, na_rpb, w_out_even, w_in_odd, gqa_sink, w_out_odd, t5_table, norm_final):
    prep = _prepare(dict(
        norm_ffn=norm_ffn, w_ffn_gate=w_ffn_gate, w_ffn_up=w_ffn_up, w_ffn_down=w_ffn_down,
        norm_mix=norm_mix, w_in_even=w_in_even, s5_lam_re=s5_lam_re, s5_lam_im=s5_lam_im,
        s5_log_dt=s5_log_dt, s5_b_re=s5_b_re, s5_b_im=s5_b_im, s5_c_re=s5_c_re, s5_c_im=s5_c_im,
        s5_d=s5_d, s5_w_glu=s5_w_glu, s5_b_glu=s5_b_glu, na_rpb=na_rpb, w_out_even=w_out_even,
        w_in_odd=w_in_odd, gqa_sink=gqa_sink, w_out_odd=w_out_odd, t5_table=t5_table,
        norm_final=norm_final))
    return (_trunk(x_prompt, prep), _trunk(x_sample, prep))
```

```python
import functools
import math

import jax
import jax.numpy as jnp
import numpy as np
from jax import lax
from jax.experimental import pallas as pl
from jax.experimental.pallas import tpu as pltpu

F32 = jnp.float32
BF16 = jnp.bfloat16

S5_GROUP = 16
S5_STATE = 64
NA_HEAD_DIM = 64
NA_ROWS = 8
NA_COLS = 16
GRID_W = 64
GQA_HEAD_DIM = 64
GQA_GROUP = 4
WINDOW = 128
BLOCK = 128
T5_BUCKETS = 32
T5_MAX_DIST = 128
RMS_EPS = 1e-6
NEG_INF = -1e30
LOG2E = math.log2(math.e)

LANES = 128
CHUNK = LANES
EVEN_TOKEN_TILE = 8 * CHUNK
TOKEN_TILE = 512
FFN_TOKEN_TILE = 1024
FFN_CHUNK = 512
NA_ROWS_PER_STEP = 8
VMEM_LIMIT = 48 * 1024 * 1024
FFN_VMEM_LIMIT = 56 * 1024 * 1024

_NT = (((1,), (1,)), ((), ()))
_HI = lax.Precision.HIGHEST


def _params(n_axes):
    return pltpu.CompilerParams(dimension_semantics=("arbitrary",) * n_axes,
                                vmem_limit_bytes=VMEM_LIMIT)


def _rms(x, g):
    return x * lax.rsqrt(jnp.mean(x * x, axis=-1, keepdims=True) + RMS_EPS) * g


def _dot(a, b):
    return jnp.dot(a, b, preferred_element_type=F32)


def _dot_nt(a, b, precision=None):
    return lax.dot_general(a, b, _NT, precision=precision, preferred_element_type=F32)


def _ffn_kernel(x_ref, g_ref, wg_ref, wu_ref, wd_ref, gf_ref, o_ref, *, final_norm, bounds):
    x = x_ref[...]
    hn = _rms(x, g_ref[...]).astype(BF16)
    acc = None
    for lo, hi in bounds:
        gate = _dot(hn, wg_ref[:, lo:hi])
        up = _dot(hn, wu_ref[:, lo:hi])
        h = (gate * jax.nn.sigmoid(gate) * up).astype(BF16)
        part = _dot(h, wd_ref[lo:hi, :])
        acc = part if acc is None else acc + part
    y = x + 0.5 * acc
    if final_norm:
        y = _rms(y, gf_ref[...])
    o_ref[...] = y


def _ffn(x, g, wg, wu, wd, g_final, final_norm):
    t, d = x.shape
    f = wg.shape[1]
    tm = FFN_TOKEN_TILE
    bounds = tuple((lo, min(lo + FFN_CHUNK, f)) for lo in range(0, f, FFN_CHUNK))
    resident = lambda shape: pl.BlockSpec(shape, lambda i: (0, 0), pipeline_mode=pl.Buffered(1))
    return pl.pallas_call(
        functools.partial(_ffn_kernel, final_norm=final_norm, bounds=bounds),
        grid=(t // tm,),
        in_specs=[
            pl.BlockSpec((tm, d), lambda i: (i, 0)),
            pl.BlockSpec((1, d), lambda i: (0, 0)),
            resident((d, f)),
            resident((d, f)),
            resident((f, d)),
            pl.BlockSpec((1, d), lambda i: (0, 0)),
        ],
        out_specs=pl.BlockSpec((tm, d), lambda i: (i, 0)),
        out_shape=jax.ShapeDtypeStruct((t, d), F32),
        compiler_params=pltpu.CompilerParams(dimension_semantics=("arbitrary",),
                                             vmem_limit_bytes=FFN_VMEM_LIMIT),
        name="ffn",
    )(x, g, wg, wu, wd, g_final)


def _inproj_kernel(x_ref, g_ref, w_ref, *o_refs, scales):
    hn = _rms(x_ref[...], g_ref[...]).astype(BF16)
    off = 0
    for o_ref, scale in zip(o_refs, scales):
        width = o_ref.shape[1]
        part = _dot(hn, w_ref[:, off:off + width])
        if scale != 1.0:
            part = part * scale
        o_ref[...] = part.astype(o_ref.dtype)
        off += width


def _inproj(x, g, w, splits):
    t, d = x.shape
    n = w.shape[1]
    tm = TOKEN_TILE
    return pl.pallas_call(
        functools.partial(_inproj_kernel, scales=tuple(s for _, _, s in splits)),
        grid=(t // tm,),
        in_specs=[
            pl.BlockSpec((tm, d), lambda i: (i, 0)),
            pl.BlockSpec((1, d), lambda i: (0, 0)),
            pl.BlockSpec((d, n), lambda i: (0, 0)),
        ],
        out_specs=[pl.BlockSpec((tm, width), lambda i: (i, 0)) for width, _, _ in splits],
        out_shape=[jax.ShapeDtypeStruct((t, width), dt) for width, dt, _ in splits],
        compiler_params=_params(1),
        name="inproj",
    )(x, g, w)


def _inproj_even_kernel(x_ref, g_ref, w_ref, ucm_ref, q_ref, k_ref, v_ref, t_scr, *, scale):
    tm = x_ref.shape[0]
    half = q_ref.shape[1]
    n_chunks = tm // CHUNK
    hn = _rms(x_ref[...], g_ref[...]).astype(BF16)
    q_ref[...] = (_dot(hn, w_ref[:, half:2 * half]) * scale).astype(q_ref.dtype)
    k_ref[...] = _dot(hn, w_ref[:, 2 * half:3 * half]).astype(k_ref.dtype)
    v_ref[...] = _dot(hn, w_ref[:, 3 * half:]).astype(v_ref.dtype)
    u = _dot(hn, w_ref[:, :half])
    for ch in range(n_chunks):
        t_scr[ch * half:(ch + 1) * half, :] = u[ch * CHUNK:(ch + 1) * CHUNK, :].T
    for j in range(half):
        grp, chan = divmod(j, S5_GROUP)
        ucm_ref[grp, :, chan * CHUNK:(chan + 1) * CHUNK] = t_scr[pl.ds(j, n_chunks, stride=half), :]


def _inproj_even(x, g, w, scale):
    t, d = x.shape
    half = w.shape[1] // 4
    groups = half // S5_GROUP
    tm = EVEN_TOKEN_TILE
    n_chunks = tm // CHUNK
    tok = lambda width: pl.BlockSpec((tm, width), lambda i: (i, 0))
    return pl.pallas_call(
        functools.partial(_inproj_even_kernel, scale=scale),
        grid=(t // tm,),
        in_specs=[tok(d), pl.BlockSpec((1, d), lambda i: (0, 0)), pl.BlockSpec(w.shape, lambda i: (0, 0))],
        out_specs=[pl.BlockSpec((groups, n_chunks, S5_GROUP * CHUNK), lambda i: (0, i, 0)),
                   tok(half), tok(half), tok(half)],
        out_shape=[jax.ShapeDtypeStruct((groups, t // CHUNK, S5_GROUP * CHUNK), F32)] +
                  [jax.ShapeDtypeStruct((t, half), BF16)] * 3,
        scratch_shapes=[pltpu.VMEM((n_chunks * half, CHUNK), F32)],
        compiler_params=_params(1),
        name="inproj_even",
    )(x, g, w)


def _cmul(ar, ai, br, bi):
    return ar * br - ai * bi, ar * bi + ai * br


def _discretise(lr, li, dt):
    mag = jnp.exp(lr * dt)
    ar = mag * jnp.cos(li * dt)
    ai = mag * jnp.sin(li * dt)
    den = lr * lr + li * li
    nr = ar - 1.0
    return ar, ai, (nr * lr + ai * li) / den, (ai * lr - nr * li) / den


def _powers(exponent, ar, ai, shape):
    pr = jnp.ones(shape, F32)
    pi = jnp.zeros(shape, F32)
    sr, si = ar, ai
    for j in range((2 * CHUNK).bit_length() - 1):
        bit = ((exponent >> j) & 1) == 1
        mr, mi = _cmul(pr, pi, sr, si)
        pr = jnp.where(bit, mr, pr)
        pi = jnp.where(bit, mi, pi)
        sr, si = _cmul(sr, si, sr, si)
    keep = exponent >= 0
    return jnp.where(keep, pr, 0.0), jnp.where(keep, pi, 0.0)


def _s5_table_kernel(lre_ref, lim_ref, ldt_ref, lre_col_ref, lim_col_ref, btr_ref, bti_ref, cr_ref, ci_ref,
                     ctr_ref, cti_ref, m_ref, zb_ref, oc_ref, aq_ref, k2_scr):
    q = CHUNK
    p = S5_STATE
    c = S5_GROUP
    lane2 = lax.broadcasted_iota(jnp.int32, (1, 2 * q), 1)
    lane1 = lax.broadcasted_iota(jnp.int32, (1, q), 1)
    row1 = lax.broadcasted_iota(jnp.int32, (q, 1), 0)

    kappa = None
    for d in range(2):
        dt = jnp.exp(ldt_ref[d, 0])
        ar, ai, zr, zi = _discretise(lre_ref[d, 0], lim_ref[d, 0], dt)
        acr, aci, _, _ = _discretise(lre_col_ref[d, 0], lim_col_ref[d, 0], dt)
        btr = btr_ref[d, 0]
        bti = bti_ref[d, 0]
        bbr = zr * btr - zi * bti
        bbi = zr * bti + zi * btr
        cre = cr_ref[d, 0]
        cim = ci_ref[d, 0]

        rep = lambda a: jnp.concatenate([jnp.broadcast_to(a[i:i + 1], (c, p)) for i in range(c)], axis=0)
        til = lambda a: jnp.concatenate([a] * c, axis=0)
        wr, wi = _cmul(rep(bbr), rep(bbi), til(cre), til(cim))
        lag = (lane2 - q) if d == 0 else (q - lane2)
        pr, pi = _powers(lag, acr, aci, (p, 2 * q))
        part = jnp.dot(wr, pr, precision=_HI, preferred_element_type=F32) - \
            jnp.dot(wi, pi, precision=_HI, preferred_element_type=F32)
        kappa = part if kappa is None else kappa + part

        zpr, zpi = _powers((q - 1 - row1) if d == 0 else row1, ar, ai, (q, p))
        for i in range(c):
            zbr, zbi = _cmul(bbr[i:i + 1], bbi[i:i + 1], zpr, zpi)
            zb_ref[0, i * q:(i + 1) * q, d * p:(d + 1) * p] = zbr.astype(BF16)
            zb_ref[0, i * q:(i + 1) * q, 2 * p + d * p:2 * p + (d + 1) * p] = zbi.astype(BF16)

        opr, opi = _powers((lane1 + 1) if d == 0 else (q - lane1), acr, aci, (p, q))
        ctr = ctr_ref[d, 0]
        cti = cti_ref[d, 0]
        for i in range(c):
            e_re, e_im = _cmul(ctr[:, i:i + 1], cti[:, i:i + 1], opr, opi)
            oc_ref[0, d * p:(d + 1) * p, i * q:(i + 1) * q] = e_re.astype(BF16)
            oc_ref[0, 2 * p + d * p:2 * p + (d + 1) * p, i * q:(i + 1) * q] = (-e_im).astype(BF16)

        aqr, aqi = _powers(jnp.full((1, 1), q, jnp.int32), ar, ai, (1, p))
        aq_ref[0, :, d * p:(d + 1) * p] = aqr
        aq_ref[0, :, 2 * p + d * p:2 * p + (d + 1) * p] = aqi

    k2_scr[...] = kappa

    def fill(i, carry):
        for j in range(c):
            row = k2_scr[pl.ds(i * c + j, 1), :]
            toe = pltpu.roll(jnp.broadcast_to(row, (q, 2 * q)), q, axis=1, stride=1, stride_axis=0)
            m_ref[0, pl.ds(pl.multiple_of(i * q, q), q), j * q:(j + 1) * q] = toe[:, :q].astype(BF16)
        return carry

    lax.fori_loop(0, c, fill, 0)


def _s5_tables(lam_re, lam_im, log_dt, b_re, b_im, c_re, c_im):
    g = lam_re.shape[1]
    p, c, q = S5_STATE, S5_GROUP, CHUNK
    swap = lambda a: jnp.swapaxes(a, -1, -2)
    spec4 = lambda r, w: pl.BlockSpec((2, 1, r, w), lambda i: (0, i, 0, 0))
    return pl.pallas_call(
        _s5_table_kernel,
        grid=(g,),
        in_specs=[spec4(1, p), spec4(1, p), spec4(1, 1), spec4(p, 1), spec4(p, 1),
                  spec4(c, p), spec4(c, p), spec4(c, p), spec4(c, p), spec4(p, c), spec4(p, c)],
        out_specs=[
            pl.BlockSpec((1, q * c, q * c), lambda i: (i, 0, 0)),
            pl.BlockSpec((1, q * c, 4 * p), lambda i: (i, 0, 0)),
            pl.BlockSpec((1, 4 * p, q * c), lambda i: (i, 0, 0)),
            pl.BlockSpec((1, 1, 4 * p), lambda i: (i, 0, 0)),
        ],
        out_shape=[
            jax.ShapeDtypeStruct((g, q * c, q * c), BF16),
            jax.ShapeDtypeStruct((g, q * c, 4 * p), BF16),
            jax.ShapeDtypeStruct((g, 4 * p, q * c), BF16),
            jax.ShapeDtypeStruct((g, 1, 4 * p), F32),
        ],
        scratch_shapes=[pltpu.VMEM((c * c, 2 * q), F32)],
        compiler_params=_params(1),
        name="s5_tables",
    )(lam_re.reshape(2, g, 1, p), lam_im.reshape(2, g, 1, p), log_dt.reshape(2, g, 1, 1),
      lam_re.reshape(2, g, p, 1), lam_im.reshape(2, g, p, 1),
      swap(b_re), swap(b_im), c_re, c_im, swap(c_re), swap(c_im))


def _s5_kernel(u_ref, d_ref, m_ref, zb_ref, oc_ref, aq_ref, y_ref,
               zr_scr, zi_scr, fr_scr, fi_scr, br_scr, bi_scr, *, nc, bsz):
    half = 2 * S5_STATE
    uf = u_ref[0]
    u = uf.astype(BF16)
    z = _dot(u, zb_ref[0])
    zr_scr[...] = z[:, :half]
    zi_scr[...] = z[:, half:]
    aq = aq_ref[0]
    ar, ai = aq[:, :half], aq[:, half:]
    zero = jnp.zeros((bsz, half), F32)

    def step(j, carry):
        fr, fi, br, bi = carry
        rows_f = pl.ds(j, bsz, stride=nc)
        rows_b = pl.ds(nc - 1 - j, bsz, stride=nc)
        fr_scr[rows_f, :] = fr
        fi_scr[rows_f, :] = fi
        br_scr[rows_b, :] = br
        bi_scr[rows_b, :] = bi
        nfr, nfi = _cmul(ar, ai, fr, fi)
        nbr, nbi = _cmul(ar, ai, br, bi)
        return (nfr + zr_scr[rows_f, :], nfi + zi_scr[rows_f, :],
                nbr + zr_scr[rows_b, :], nbi + zi_scr[rows_b, :])

    lax.fori_loop(0, nc, step, (zero, zero, zero, zero))
    is_fwd = lax.broadcasted_iota(jnp.int32, (1, half), 1) < S5_STATE
    s = jnp.concatenate([jnp.where(is_fwd, fr_scr[...], br_scr[...]),
                         jnp.where(is_fwd, fi_scr[...], bi_scr[...])], axis=1).astype(BF16)
    y_ref[0] = d_ref[0] * uf + _dot(u, m_ref[0]) + _dot(s, oc_ref[0])


def _s5_apply(u_cm, d_cm, m, zb, oc, aq, nc, bsz):
    g, rows, w = u_cm.shape
    st = zb.shape[2]
    return pl.pallas_call(
        functools.partial(_s5_kernel, nc=nc, bsz=bsz),
        grid=(g,),
        in_specs=[
            pl.BlockSpec((1, rows, w), lambda i: (i, 0, 0)),
            pl.BlockSpec((1, 1, w), lambda i: (i, 0, 0)),
            pl.BlockSpec((1, w, w), lambda i: (i, 0, 0)),
            pl.BlockSpec((1, w, st), lambda i: (i, 0, 0)),
            pl.BlockSpec((1, st, w), lambda i: (i, 0, 0)),
            pl.BlockSpec((1, 1, st), lambda i: (i, 0, 0)),
        ],
        out_specs=pl.BlockSpec((1, rows, w), lambda i: (i, 0, 0)),
        out_shape=jax.ShapeDtypeStruct((g, rows, w), F32),
        scratch_shapes=[pltpu.VMEM((rows, st // 2), F32)] * 6,
        compiler_params=_params(1),
        name="s5_apply",
    )(u_cm, d_cm, m, zb, oc, aq)


def _na_bias_kernel(rpb_ref, o_ref):
    st = pl.program_id(0)
    h = pl.program_id(1)
    n_rel_rows = 2 * NA_ROWS - 1
    n_rel_cols = 2 * NA_COLS - 1
    w = lax.broadcasted_iota(jnp.int32, (GRID_W, LANES), 0)
    lane = lax.broadcasted_iota(jnp.int32, (GRID_W, LANES), 1)
    wk = lane & (GRID_W - 1)
    first = lane < GRID_W
    cs = jnp.clip(w - NA_COLS // 2, 0, GRID_W - NA_COLS)
    valid = (wk >= cs) & (wk < cs + NA_COLS)
    rel = wk - w + NA_COLS - 1
    for i in range(NA_ROWS * GRID_W // LANES):
        base0 = (h * n_rel_rows + st + 2 * i) * n_rel_cols
        base1 = base0 + n_rel_cols
        acc = jnp.full((GRID_W, LANES), NEG_INF, F32)
        for j in range(n_rel_cols):
            val = jnp.where(first, rpb_ref[base0 + j], rpb_ref[base1 + j]) * LOG2E
            acc = jnp.where(valid & (rel == j), val, acc)
        o_ref[0, 0, :, i * LANES:(i + 1) * LANES] = acc


def _na_bias(rpb):
    h = rpb.shape[0]
    pairs = h // 2
    return pl.pallas_call(
        _na_bias_kernel,
        grid=(NA_ROWS, h),
        in_specs=[pl.BlockSpec(memory_space=pltpu.SMEM)],
        out_specs=pl.BlockSpec((1, 1, GRID_W, NA_ROWS * GRID_W), lambda s, i: (s, i // 2, i % 2, 0)),
        out_shape=jax.ShapeDtypeStruct((NA_ROWS, pairs, 2 * GRID_W, NA_ROWS * GRID_W), F32),
        compiler_params=_params(2),
        name="na_bias",
    )(rpb.reshape(-1))


def _na_kernel(q_ref, k_ref, v_ref, b_ref, o_ref, s_scr, p_scr, *, rows):
    lane = lax.broadcasted_iota(jnp.int32, (GRID_W, LANES), 1)
    first = lane < NA_HEAD_DIM
    n_keys = NA_ROWS * GRID_W
    ones = jnp.ones((n_keys, LANES), BF16)

    def row_start(r):
        return jnp.clip(r - NA_ROWS // 2, 0, rows - NA_ROWS)

    def row_group(i, carry):
        base = i * NA_ROWS_PER_STEP
        for j in range(NA_ROWS_PER_STEP):
            r = base + j
            rs = row_start(r)
            q = q_ref[0, r]
            zero = jnp.zeros_like(q)
            qs = jnp.concatenate([jnp.where(first, q, zero), jnp.where(first, zero, q)], axis=0)
            kw = k_ref[0, pl.ds(rs, NA_ROWS)].reshape(n_keys, LANES)
            s_scr[j] = _dot_nt(qs, kw) + b_ref[rs - r + NA_ROWS - 1, 0]
        s = s_scr[...]
        p_scr[...] = jnp.exp2(s - jnp.max(s, axis=-1, keepdims=True)).astype(BF16)
        for j in range(NA_ROWS_PER_STEP):
            r = base + j
            vw = v_ref[0, pl.ds(row_start(r), NA_ROWS)].reshape(n_keys, LANES)
            on = _dot(p_scr[j], jnp.concatenate([vw, ones], axis=1))
            o = on[:, :LANES] / on[:, LANES:]
            o_ref[0, r] = jnp.where(first, o[:GRID_W], o[GRID_W:]).astype(o_ref.dtype)
        return carry

    lax.fori_loop(0, rows // NA_ROWS_PER_STEP, row_group, 0)


def _na(q, k, v, bias, bsz, rows):
    t, width = q.shape
    pairs = width // LANES
    shape4 = (bsz, rows, GRID_W, width)
    blk = pl.BlockSpec((1, rows, GRID_W, LANES), lambda b, i: (b, 0, 0, i))
    out = pl.pallas_call(
        functools.partial(_na_kernel, rows=rows),
        grid=(bsz, pairs),
        in_specs=[blk, blk, blk,
                  pl.BlockSpec((NA_ROWS, 1, 2 * GRID_W, NA_ROWS * GRID_W), lambda b, i: (0, i, 0, 0))],
        out_specs=blk,
        out_shape=jax.ShapeDtypeStruct(shape4, BF16),
        scratch_shapes=[pltpu.VMEM((NA_ROWS_PER_STEP, 2 * GRID_W, NA_ROWS * GRID_W), F32),
                        pltpu.VMEM((NA_ROWS_PER_STEP, 2 * GRID_W, NA_ROWS * GRID_W), BF16)],
        compiler_params=_params(2),
        name="na_attn",
    )(q.reshape(shape4), k.reshape(shape4), v.reshape(shape4), bias)
    return out.reshape(t, width)


def _out_even_kernel(ycm_ref, na_ref, x_ref, wglu_ref, bglu_ref, wo_ref, o_ref, t_scr):
    groups, n_chunks, _ = ycm_ref.shape
    half = groups * S5_GROUP
    for j in range(half):
        grp, chan = divmod(j, S5_GROUP)
        t_scr[pl.ds(j, n_chunks, stride=half), :] = ycm_ref[grp, :, chan * CHUNK:(chan + 1) * CHUNK]
    y = jnp.concatenate([t_scr[ch * half:(ch + 1) * half, :].T for ch in range(n_chunks)], axis=0)
    g = jax.nn.gelu(y)
    a = g * jax.nn.sigmoid(_dot(g.astype(BF16), wglu_ref[...]) + bglu_ref[...])
    mix = _dot(a.astype(BF16), wo_ref[:half, :]) + _dot(na_ref[...], wo_ref[half:, :])
    o_ref[...] = x_ref[...] + mix


def _out_even(y_cm, na, x, w_glu, b_glu, w_out):
    t, d = x.shape
    groups = y_cm.shape[0]
    half = groups * S5_GROUP
    tm = EVEN_TOKEN_TILE
    n_chunks = tm // CHUNK
    row = lambda w: pl.BlockSpec((tm, w), lambda i: (i, 0))
    full = lambda r, w: pl.BlockSpec((r, w), lambda i: (0, 0))
    return pl.pallas_call(
        _out_even_kernel,
        grid=(t // tm,),
        in_specs=[pl.BlockSpec((groups, n_chunks, S5_GROUP * CHUNK), lambda i: (0, i, 0)),
                  row(half), row(d), full(half, half), full(1, half), full(d, d)],
        out_specs=row(d),
        out_shape=jax.ShapeDtypeStruct((t, d), F32),
        scratch_shapes=[pltpu.VMEM((n_chunks * half, CHUNK), F32)],
        compiler_params=_params(1),
        name="out_even",
    )(y_cm, na, x, w_glu, b_glu, w_out)


def _t5_index_table():
    half = T5_BUCKETS // 2
    max_exact = half // 2
    rel = np.arange(3 * BLOCK)[None, :] - BLOCK - np.arange(BLOCK)[:, None]
    n = np.abs(rel)
    assert T5_MAX_DIST // max_exact == 16 and half - max_exact == 8 and max_exact == 8
    sq = np.maximum(n, 1).astype(np.int64) ** 2
    log2_sq = np.floor(np.log2(sq.astype(np.float64) + 0.5)).astype(np.int64)
    large = np.minimum(max_exact + log2_sq - 6, half - 1)
    bucket = np.where(rel > 0, half, 0) + np.where(n < max_exact, n, large)
    return np.where(n <= WINDOW, bucket, -1).astype(np.int32)


def _t5_bias_kernel(idx_ref, t5_ref, o_ref, *, heads):
    variant = pl.program_id(0)
    h = pl.program_id(1)
    idx = idx_ref[...]
    acc = jnp.full(idx.shape, NEG_INF, F32)
    for b in range(T5_BUCKETS):
        acc = jnp.where(idx == b, t5_ref[b * heads + h] * LOG2E, acc)
    col = lax.broadcasted_iota(jnp.int32, idx.shape, 1)
    absent = ((variant == 1) & (col < BLOCK)) | ((variant == 2) & (col >= 2 * BLOCK))
    o_ref[0, 0] = jnp.where(absent, NEG_INF, acc)


def _t5_bias(t5_table):
    heads = t5_table.shape[1]
    idx = jnp.asarray(_t5_index_table())
    per_pair = 2 * GQA_GROUP
    return pl.pallas_call(
        functools.partial(_t5_bias_kernel, heads=heads),
        grid=(3, heads),
        in_specs=[pl.BlockSpec((BLOCK, 3 * BLOCK), lambda e, h: (0, 0)),
                  pl.BlockSpec(memory_space=pltpu.SMEM)],
        out_specs=pl.BlockSpec(
            (1, 1, BLOCK, 3 * BLOCK),
            lambda e, h: (e, (h // per_pair) * GQA_GROUP + h % GQA_GROUP, (h % per_pair) // GQA_GROUP, 0)),
        out_shape=jax.ShapeDtypeStruct((3, heads // 2, 2 * BLOCK, 3 * BLOCK), F32),
        compiler_params=_params(2),
        name="t5_bias",
    )(idx, t5_table.reshape(-1))


def _gqa_head_order(heads):
    order = []
    for tt in range(heads // 2):
        pair, t = divmod(tt, GQA_GROUP)
        order += [2 * GQA_GROUP * pair + t, 2 * GQA_GROUP * pair + GQA_GROUP + t]
    return order


def _gqa_kernel(q_ref, kp_ref, kc_ref, kn_ref, vp_ref, vc_ref, vn_ref, bias_ref, sink_ref, o_ref, *, heads):
    k = jnp.concatenate([kp_ref[...], kc_ref[...], kn_ref[...]], axis=0)
    v = jnp.concatenate([vp_ref[...], vc_ref[...], vn_ref[...]], axis=0)
    lane = lax.broadcasted_iota(jnp.int32, (BLOCK, LANES), 1)
    first = lane < GQA_HEAD_DIM
    upper = lax.broadcasted_iota(jnp.int32, (2 * BLOCK, 1), 0) < BLOCK
    ones = jnp.ones((3 * BLOCK, LANES), BF16)
    order = _gqa_head_order(heads)
    for tt in range(heads // 2):
        pair = tt // GQA_GROUP
        k2 = k[:, pair * LANES:(pair + 1) * LANES]
        v2 = jnp.concatenate([v[:, pair * LANES:(pair + 1) * LANES], ones], axis=1)
        tile = q_ref[:, tt * LANES:(tt + 1) * LANES]
        zero = jnp.zeros_like(tile)
        qs = jnp.concatenate([jnp.where(first, tile, zero), jnp.where(first, zero, tile)], axis=0)
        s = _dot_nt(qs, k2) + bias_ref[0, tt]
        sk = jnp.where(upper, sink_ref[order[2 * tt]], sink_ref[order[2 * tt + 1]]) * LOG2E
        m = jnp.maximum(jnp.max(s, axis=-1, keepdims=True), sk)
        on = _dot(jnp.exp2(s - m).astype(BF16), v2)
        o = on[:, :LANES] / (on[:, LANES:] + jnp.exp2(sk - m))
        o_ref[:, tt * LANES:(tt + 1) * LANES] = jnp.where(first, o[:BLOCK], o[BLOCK:]).astype(o_ref.dtype)


def _gqa(q, k, v, bias, sink, bsz, nb):
    t, qw = q.shape
    kw = k.shape[1]
    heads = qw // GQA_HEAD_DIM
    cur = lambda w: pl.BlockSpec((BLOCK, w), lambda b, n: (b * nb + n, 0))
    prev = lambda w: pl.BlockSpec((BLOCK, w), lambda b, n: (b * nb + jnp.maximum(n - 1, 0), 0))
    nxt = lambda w: pl.BlockSpec((BLOCK, w), lambda b, n: (b * nb + jnp.minimum(n + 1, nb - 1), 0))
    assert nb >= 2, "a sequence shorter than two blocks has no separate first / last block variant"
    variant = lambda n: jnp.where(n == 0, 1, jnp.where(n == nb - 1, 2, 0))
    return pl.pallas_call(
        functools.partial(_gqa_kernel, heads=heads),
        grid=(bsz, nb),
        in_specs=[cur(qw), prev(kw), cur(kw), nxt(kw), prev(kw), cur(kw), nxt(kw),
                  pl.BlockSpec((1, heads // 2, 2 * BLOCK, 3 * BLOCK), lambda b, n: (variant(n), 0, 0, 0)),
                  pl.BlockSpec(memory_space=pltpu.SMEM)],
        out_specs=cur(qw),
        out_shape=jax.ShapeDtypeStruct((t, qw), BF16),
        compiler_params=_params(2),
        name="gqa_attn",
    )(q, k, k, k, v, v, v, bias, sink)


def _out_odd_kernel(a_ref, x_ref, wo_ref, o_ref):
    o_ref[...] = x_ref[...] + _dot(a_ref[...], wo_ref[...])


def _out_odd(a, x, w_out):
    t, d = x.shape
    tm = TOKEN_TILE
    return pl.pallas_call(
        _out_odd_kernel,
        grid=(t // tm,),
        in_specs=[pl.BlockSpec((tm, a.shape[1]), lambda i: (i, 0)),
                  pl.BlockSpec((tm, d), lambda i: (i, 0)),
                  pl.BlockSpec(w_out.shape, lambda i: (0, 0))],
        out_specs=pl.BlockSpec((tm, d), lambda i: (i, 0)),
        out_shape=jax.ShapeDtypeStruct((t, d), F32),
        compiler_params=_params(1),
        name="out_odd",
    )(a, x, w_out)


def _prepare(p):
    depth = p["norm_mix"].shape[0]
    row = lambda a: a.reshape(1, -1).astype(F32)
    prep = {"depth": depth, "layers": []}
    heads = p["gqa_sink"].shape[-1] if depth > 1 else 0
    if depth > 1:
        order = np.asarray(_gqa_head_order(heads))
        slot_cols = (order[:, None] * GQA_HEAD_DIM + np.arange(GQA_HEAD_DIM)[None, :]).reshape(-1)
        prep["t5_bias"] = _t5_bias(p["t5_table"].astype(F32))
    for layer in range(depth):
        i = layer // 2
        lp = {
            "ffn": [(row(p["norm_ffn"][layer, j]), p["w_ffn_gate"][layer, j].astype(BF16),
                     p["w_ffn_up"][layer, j].astype(BF16), p["w_ffn_down"][layer, j].astype(BF16))
                    for j in range(2)],
            "norm_mix": row(p["norm_mix"][layer]),
        }
        if layer % 2 == 0:
            lp["w_in"] = p["w_in_even"][i].astype(BF16)
            lp["tables"] = _s5_tables(p["s5_lam_re"][i].astype(F32), p["s5_lam_im"][i].astype(F32),
                                      p["s5_log_dt"][i].astype(F32), p["s5_b_re"][i].astype(F32),
                                      p["s5_b_im"][i].astype(F32), p["s5_c_re"][i].astype(F32),
                                      p["s5_c_im"][i].astype(F32))
            groups = p["s5_lam_re"].shape[2]
            lp["d_skip"] = jnp.repeat(p["s5_d"][i].astype(F32).reshape(groups, 1, S5_GROUP), CHUNK, axis=2)
            lp["w_glu"] = p["s5_w_glu"][i].astype(BF16)
            lp["b_glu"] = row(p["s5_b_glu"][i])
            lp["na_bias"] = _na_bias(p["na_rpb"][i].astype(F32))
            lp["w_out"] = p["w_out_even"][i].astype(BF16)
        else:
            w_in = p["w_in_odd"][i]
            nq = heads * GQA_HEAD_DIM
            lp["w_in"] = jnp.concatenate([w_in[:, slot_cols], w_in[:, nq:]], axis=1).astype(BF16)
            lp["sink"] = p["gqa_sink"][i].astype(F32)
            lp["w_out"] = p["w_out_odd"][i][slot_cols, :].astype(BF16)
        prep["layers"].append(lp)
    prep["norm_final"] = row(p["norm_final"])
    return prep


def _trunk(x, prep):
    bsz, seq, d = x.shape
    t = bsz * seq
    x = x.reshape(t, d).astype(F32)
    depth = prep["depth"]
    for layer, lp in enumerate(prep["layers"]):
        x = _ffn(x, *lp["ffn"][0], prep["norm_final"], False)
        if layer % 2 == 0:
            u_cm, q, k, v = _inproj_even(x, lp["norm_mix"], lp["w_in"], NA_HEAD_DIM ** -0.5 * LOG2E)
            y_cm = _s5_apply(u_cm, lp["d_skip"], *lp["tables"], seq // CHUNK, bsz)
            na = _na(q, k, v, lp["na_bias"], bsz, seq // GRID_W)
            x = _out_even(y_cm, na, x, lp["w_glu"], lp["b_glu"], lp["w_out"])
        else:
            heads = lp["sink"].shape[0]
            nq = heads * GQA_HEAD_DIM
            nkv = nq // GQA_GROUP
            scale = GQA_HEAD_DIM ** -0.5 * LOG2E
            q, k, v = _inproj(x, lp["norm_mix"], lp["w_in"],
                              [(nq, BF16, scale), (nkv, BF16, 1.0), (nkv, BF16, 1.0)])
            a = _gqa(q, k, v, prep["t5_bias"], lp["sink"], bsz, seq // BLOCK)
            x = _out_odd(a, x, lp["w_out"])
        x = _ffn(x, *lp["ffn"][1], prep["norm_final"], layer == depth - 1)
    return x.reshape(bsz, seq, d)


def kernel(x_prompt, x_sample, norm_ffn, w_ffn_gate, w_ffn_up, w_ffn_down, norm_mix, w_in_even, s5_lam_re, s5_lam_im, s5_log_dt, s5_b_re, s5_b_im, s5_c_re, s5_c_im, s5_d, s5_w_glu, s5_b_glu, na_rpb, w_out_even, w_in_odd, gqa_sink, w_out_odd, t5_table, norm_final):
    prep = _prepare(dict(
        norm_ffn=norm_ffn, w_ffn_gate=w_ffn_gate, w_ffn_up=w_ffn_up, w_ffn_down=w_ffn_down,
        norm_mix=norm_mix, w_in_even=w_in_even, s5_lam_re=s5_lam_re, s5_lam_im=s5_lam_im,
        s5_log_dt=s5_log_dt, s5_b_re=s5_b_re, s5_b_im=s5_b_im, s5_c_re=s5_c_re, s5_c_im=s5_c_im,
        s5_d=s5_d, s5_w_glu=s5_w_glu, s5_b_glu=s5_b_glu, na_rpb=na_rpb, w_out_even=w_out_even,
        w_in_odd=w_in_odd, gqa_sink=gqa_sink, w_out_odd=w_out_odd, t5_table=t5_table,
        norm_final=norm_final))
    return (_trunk(x_prompt, prep), _trunk(x_sample, prep))
```

```python
import functools
import math

import jax
import jax.numpy as jnp
import numpy as np
from jax import lax
from jax.experimental import pallas as pl
from jax.experimental.pallas import tpu as pltpu

F32 = jnp.float32
BF16 = jnp.bfloat16

S5_GROUP = 16
S5_STATE = 64
NA_HEAD_DIM = 64
NA_ROWS = 8
NA_COLS = 16
GRID_W = 64
GQA_HEAD_DIM = 64
GQA_GROUP = 4
WINDOW = 128
BLOCK = 128
T5_BUCKETS = 32
T5_MAX_DIST = 128
RMS_EPS = 1e-6
NEG_INF = -1e30
LOG2E = math.log2(math.e)

LANES = 128
CHUNK = LANES
EVEN_TOKEN_TILE = 8 * CHUNK
TOKEN_TILE = 512
FFN_TOKEN_TILE = 1024
FFN_CHUNK = 512
NA_ROWS_PER_STEP = 8
VMEM_LIMIT = 48 * 1024 * 1024
BIG_VMEM_LIMIT = 60 * 1024 * 1024

_NT = (((1,), (1,)), ((), ()))
_HI = lax.Precision.HIGHEST


def _params(n_axes):
    return pltpu.CompilerParams(dimension_semantics=("arbitrary",) * n_axes,
                                vmem_limit_bytes=VMEM_LIMIT)


def _rms(x, g):
    return x * lax.rsqrt(jnp.mean(x * x, axis=-1, keepdims=True) + RMS_EPS) * g


def _dot(a, b):
    return jnp.dot(a, b, preferred_element_type=F32)


def _dot_nt(a, b, precision=None):
    return lax.dot_general(a, b, _NT, precision=precision, preferred_element_type=F32)


def _ffn_body(x, g_ref, wg_ref, wu_ref, wd_ref, gf_ref, o_ref, final_norm, bounds):
    hn = _rms(x, g_ref[...]).astype(BF16)
    acc = None
    for lo, hi in bounds:
        gate = _dot(hn, wg_ref[:, lo:hi])
        up = _dot(hn, wu_ref[:, lo:hi])
        h = (gate * jax.nn.sigmoid(gate) * up).astype(BF16)
        part = _dot(h, wd_ref[lo:hi, :])
        acc = part if acc is None else acc + part
    y = x + 0.5 * acc
    if final_norm:
        y = _rms(y, gf_ref[...])
    o_ref[...] = y


def _ffn_kernel(x_ref, g_ref, wg_ref, wu_ref, wd_ref, gf_ref, o_ref, *, final_norm, bounds):
    _ffn_body(x_ref[...], g_ref, wg_ref, wu_ref, wd_ref, gf_ref, o_ref, final_norm, bounds)


def _proj_ffn_kernel(a_ref, wo_ref, x_ref, g_ref, wg_ref, wu_ref, wd_ref, gf_ref, o_ref, *, final_norm, bounds):
    x = x_ref[...] + _dot(a_ref[...], wo_ref[...])
    _ffn_body(x, g_ref, wg_ref, wu_ref, wd_ref, gf_ref, o_ref, final_norm, bounds)


def _ffn(x, g, wg, wu, wd, g_final, final_norm, mixed=None):
    t, d = x.shape
    f = wg.shape[1]
    tm = FFN_TOKEN_TILE
    bounds = tuple((lo, min(lo + FFN_CHUNK, f)) for lo in range(0, f, FFN_CHUNK))
    resident = lambda shape: pl.BlockSpec(shape, lambda i: (0, 0), pipeline_mode=pl.Buffered(1))
    row = lambda width: pl.BlockSpec((tm, width), lambda i: (i, 0))
    vec = pl.BlockSpec((1, d), lambda i: (0, 0))
    in_specs = [row(d), vec, resident((d, f)), resident((d, f)), resident((f, d)), vec]
    args = (x, g, wg, wu, wd, g_final)
    body = _ffn_kernel
    if mixed is not None:
        a, w_out = mixed
        in_specs = [row(a.shape[1]), resident(w_out.shape)] + in_specs
        args = (a, w_out) + args
        body = _proj_ffn_kernel
    return pl.pallas_call(
        functools.partial(body, final_norm=final_norm, bounds=bounds),
        grid=(t // tm,),
        in_specs=in_specs,
        out_specs=row(d),
        out_shape=jax.ShapeDtypeStruct((t, d), F32),
        compiler_params=pltpu.CompilerParams(dimension_semantics=("arbitrary",),
                                             vmem_limit_bytes=BIG_VMEM_LIMIT),
        name="ffn" if mixed is None else "proj_ffn",
    )(*args)


def _inproj_kernel(x_ref, g_ref, w_ref, *o_refs, scales):
    hn = _rms(x_ref[...], g_ref[...]).astype(BF16)
    off = 0
    for o_ref, scale in zip(o_refs, scales):
        width = o_ref.shape[1]
        part = _dot(hn, w_ref[:, off:off + width])
        if scale != 1.0:
            part = part * scale
        o_ref[...] = part.astype(o_ref.dtype)
        off += width


def _inproj(x, g, w, splits):
    t, d = x.shape
    n = w.shape[1]
    tm = TOKEN_TILE
    return pl.pallas_call(
        functools.partial(_inproj_kernel, scales=tuple(s for _, _, s in splits)),
        grid=(t // tm,),
        in_specs=[
            pl.BlockSpec((tm, d), lambda i: (i, 0)),
            pl.BlockSpec((1, d), lambda i: (0, 0)),
            pl.BlockSpec((d, n), lambda i: (0, 0)),
        ],
        out_specs=[pl.BlockSpec((tm, width), lambda i: (i, 0)) for width, _, _ in splits],
        out_shape=[jax.ShapeDtypeStruct((t, width), dt) for width, dt, _ in splits],
        compiler_params=_params(1),
        name="inproj",
    )(x, g, w)


def _inproj_even_kernel(x_ref, g_ref, w_ref, ucm_ref, q_ref, k_ref, v_ref, t_scr, *, scale):
    tm = x_ref.shape[0]
    half = q_ref.shape[1]
    n_chunks = tm // CHUNK
    hn = _rms(x_ref[...], g_ref[...]).astype(BF16)
    u = _dot(hn, w_ref[:, :half])
    for ch in range(n_chunks):
        t_scr[ch * half:(ch + 1) * half, :] = u[ch * CHUNK:(ch + 1) * CHUNK, :].T
    q_ref[...] = (_dot(hn, w_ref[:, half:2 * half]) * scale).astype(q_ref.dtype)
    k_ref[...] = _dot(hn, w_ref[:, 2 * half:3 * half]).astype(k_ref.dtype)
    v_ref[...] = _dot(hn, w_ref[:, 3 * half:]).astype(v_ref.dtype)
    for j in range(half):
        grp, chan = divmod(j, S5_GROUP)
        ucm_ref[grp, :, chan * CHUNK:(chan + 1) * CHUNK] = t_scr[pl.ds(j, n_chunks, stride=half), :]


def _inproj_even(x, g, w, scale):
    t, d = x.shape
    half = w.shape[1] // 4
    groups = half // S5_GROUP
    tm = EVEN_TOKEN_TILE
    n_chunks = tm // CHUNK
    tok = lambda width: pl.BlockSpec((tm, width), lambda i: (i, 0))
    return pl.pallas_call(
        functools.partial(_inproj_even_kernel, scale=scale),
        grid=(t // tm,),
        in_specs=[tok(d), pl.BlockSpec((1, d), lambda i: (0, 0)), pl.BlockSpec(w.shape, lambda i: (0, 0))],
        out_specs=[pl.BlockSpec((groups, n_chunks, S5_GROUP * CHUNK), lambda i: (0, i, 0)),
                   tok(half), tok(half), tok(half)],
        out_shape=[jax.ShapeDtypeStruct((groups, t // CHUNK, S5_GROUP * CHUNK), F32)] +
                  [jax.ShapeDtypeStruct((t, half), BF16)] * 3,
        scratch_shapes=[pltpu.VMEM((n_chunks * half, CHUNK), F32)],
        compiler_params=_params(1),
        name="inproj_even",
    )(x, g, w)


def _cmul(ar, ai, br, bi):
    return ar * br - ai * bi, ar * bi + ai * br


def _discretise(lr, li, dt):
    mag = jnp.exp(lr * dt)
    ar = mag * jnp.cos(li * dt)
    ai = mag * jnp.sin(li * dt)
    den = lr * lr + li * li
    nr = ar - 1.0
    return ar, ai, (nr * lr + ai * li) / den, (ai * lr - nr * li) / den


def _powers(exponent, ar, ai, shape):
    pr = jnp.ones(shape, F32)
    pi = jnp.zeros(shape, F32)
    sr, si = ar, ai
    for j in range((2 * CHUNK).bit_length() - 1):
        bit = ((exponent >> j) & 1) == 1
        mr, mi = _cmul(pr, pi, sr, si)
        pr = jnp.where(bit, mr, pr)
        pi = jnp.where(bit, mi, pi)
        sr, si = _cmul(sr, si, sr, si)
    keep = exponent >= 0
    return jnp.where(keep, pr, 0.0), jnp.where(keep, pi, 0.0)


def _s5_table_kernel(lre_ref, lim_ref, ldt_ref, lre_col_ref, lim_col_ref, btr_ref, bti_ref, cr_ref, ci_ref,
                     ctr_ref, cti_ref, m_ref, zb_ref, oc_ref, aq_ref, k2_scr):
    q = CHUNK
    p = S5_STATE
    c = S5_GROUP
    lane2 = lax.broadcasted_iota(jnp.int32, (1, 2 * q), 1)
    lane1 = lax.broadcasted_iota(jnp.int32, (1, q), 1)
    row1 = lax.broadcasted_iota(jnp.int32, (q, 1), 0)

    kappa = None
    for d in range(2):
        dt = jnp.exp(ldt_ref[d, 0])
        ar, ai, zr, zi = _discretise(lre_ref[d, 0], lim_ref[d, 0], dt)
        acr, aci, _, _ = _discretise(lre_col_ref[d, 0], lim_col_ref[d, 0], dt)
        btr = btr_ref[d, 0]
        bti = bti_ref[d, 0]
        bbr = zr * btr - zi * bti
        bbi = zr * bti + zi * btr
        cre = cr_ref[d, 0]
        cim = ci_ref[d, 0]

        rep = lambda a: jnp.concatenate([jnp.broadcast_to(a[i:i + 1], (c, p)) for i in range(c)], axis=0)
        til = lambda a: jnp.concatenate([a] * c, axis=0)
        wr, wi = _cmul(rep(bbr), rep(bbi), til(cre), til(cim))
        lag = (lane2 - q) if d == 0 else (q - lane2)
        pr, pi = _powers(lag, acr, aci, (p, 2 * q))
        part = jnp.dot(wr, pr, precision=_HI, preferred_element_type=F32) - \
            jnp.dot(wi, pi, precision=_HI, preferred_element_type=F32)
        kappa = part if kappa is None else kappa + part

        zpr, zpi = _powers((q - 1 - row1) if d == 0 else row1, ar, ai, (q, p))
        for i in range(c):
            zbr, zbi = _cmul(bbr[i:i + 1], bbi[i:i + 1], zpr, zpi)
            zb_ref[0, i * q:(i + 1) * q, d * p:(d + 1) * p] = zbr.astype(BF16)
            zb_ref[0, i * q:(i + 1) * q, 2 * p + d * p:2 * p + (d + 1) * p] = zbi.astype(BF16)

        opr, opi = _powers((lane1 + 1) if d == 0 else (q - lane1), acr, aci, (p, q))
        ctr = ctr_ref[d, 0]
        cti = cti_ref[d, 0]
        for i in range(c):
            e_re, e_im = _cmul(ctr[:, i:i + 1], cti[:, i:i + 1], opr, opi)
            oc_ref[0, d * p:(d + 1) * p, i * q:(i + 1) * q] = e_re.astype(BF16)
            oc_ref[0, 2 * p + d * p:2 * p + (d + 1) * p, i * q:(i + 1) * q] = (-e_im).astype(BF16)

        aqr, aqi = _powers(jnp.full((1, 1), q, jnp.int32), ar, ai, (1, p))
        aq_ref[0, :, d * p:(d + 1) * p] = aqr
        aq_ref[0, :, 2 * p + d * p:2 * p + (d + 1) * p] = aqi

    k2_scr[...] = kappa

    def fill(i, carry):
        for j in range(c):
            row = k2_scr[pl.ds(i * c + j, 1), :]
            toe = pltpu.roll(jnp.broadcast_to(row, (q, 2 * q)), q, axis=1, stride=1, stride_axis=0)
            m_ref[0, pl.ds(pl.multiple_of(i * q, q), q), j * q:(j + 1) * q] = toe[:, :q].astype(BF16)
        return carry

    lax.fori_loop(0, c, fill, 0)


def _s5_tables(lam_re, lam_im, log_dt, b_re, b_im, c_re, c_im):
    g = lam_re.shape[1]
    p, c, q = S5_STATE, S5_GROUP, CHUNK
    swap = lambda a: jnp.swapaxes(a, -1, -2)
    spec4 = lambda r, w: pl.BlockSpec((2, 1, r, w), lambda i: (0, i, 0, 0))
    return pl.pallas_call(
        _s5_table_kernel,
        grid=(g,),
        in_specs=[spec4(1, p), spec4(1, p), spec4(1, 1), spec4(p, 1), spec4(p, 1),
                  spec4(c, p), spec4(c, p), spec4(c, p), spec4(c, p), spec4(p, c), spec4(p, c)],
        out_specs=[
            pl.BlockSpec((1, q * c, q * c), lambda i: (i, 0, 0)),
            pl.BlockSpec((1, q * c, 4 * p), lambda i: (i, 0, 0)),
            pl.BlockSpec((1, 4 * p, q * c), lambda i: (i, 0, 0)),
            pl.BlockSpec((1, 1, 4 * p), lambda i: (i, 0, 0)),
        ],
        out_shape=[
            jax.ShapeDtypeStruct((g, q * c, q * c), BF16),
            jax.ShapeDtypeStruct((g, q * c, 4 * p), BF16),
            jax.ShapeDtypeStruct((g, 4 * p, q * c), BF16),
            jax.ShapeDtypeStruct((g, 1, 4 * p), F32),
        ],
        scratch_shapes=[pltpu.VMEM((c * c, 2 * q), F32)],
        compiler_params=_params(1),
        name="s5_tables",
    )(lam_re.reshape(2, g, 1, p), lam_im.reshape(2, g, 1, p), log_dt.reshape(2, g, 1, 1),
      lam_re.reshape(2, g, p, 1), lam_im.reshape(2, g, p, 1),
      swap(b_re), swap(b_im), c_re, c_im, swap(c_re), swap(c_im))


def _s5_kernel(u_ref, d_ref, m_ref, zb_ref, oc_ref, aq_ref, y_ref,
               zr_scr, zi_scr, fr_scr, fi_scr, br_scr, bi_scr, *, nc, bsz):
    half = 2 * S5_STATE
    uf = u_ref[0]
    u = uf.astype(BF16)
    z = _dot(u, zb_ref[0])
    zr_scr[...] = z[:, :half]
    zi_scr[...] = z[:, half:]
    aq = aq_ref[0]
    ar, ai = aq[:, :half], aq[:, half:]
    zero = jnp.zeros((bsz, half), F32)

    def step(j, carry):
        fr, fi, br, bi = carry
        rows_f = pl.ds(j, bsz, stride=nc)
        rows_b = pl.ds(nc - 1 - j, bsz, stride=nc)
        fr_scr[rows_f, :] = fr
        fi_scr[rows_f, :] = fi
        br_scr[rows_b, :] = br
        bi_scr[rows_b, :] = bi
        nfr, nfi = _cmul(ar, ai, fr, fi)
        nbr, nbi = _cmul(ar, ai, br, bi)
        return (nfr + zr_scr[rows_f, :], nfi + zi_scr[rows_f, :],
                nbr + zr_scr[rows_b, :], nbi + zi_scr[rows_b, :])

    lax.fori_loop(0, nc, step, (zero, zero, zero, zero))
    is_fwd = lax.broadcasted_iota(jnp.int32, (1, half), 1) < S5_STATE
    s = jnp.concatenate([jnp.where(is_fwd, fr_scr[...], br_scr[...]),
                         jnp.where(is_fwd, fi_scr[...], bi_scr[...])], axis=1).astype(BF16)
    y_ref[0] = d_ref[0] * uf + _dot(u, m_ref[0]) + _dot(s, oc_ref[0])


def _s5_apply(u_cm, d_cm, m, zb, oc, aq, nc, bsz):
    g, rows, w = u_cm.shape
    st = zb.shape[2]
    return pl.pallas_call(
        functools.partial(_s5_kernel, nc=nc, bsz=bsz),
        grid=(g,),
        in_specs=[
            pl.BlockSpec((1, rows, w), lambda i: (i, 0, 0)),
            pl.BlockSpec((1, 1, w), lambda i: (i, 0, 0)),
            pl.BlockSpec((1, w, w), lambda i: (i, 0, 0)),
            pl.BlockSpec((1, w, st), lambda i: (i, 0, 0)),
            pl.BlockSpec((1, st, w), lambda i: (i, 0, 0)),
            pl.BlockSpec((1, 1, st), lambda i: (i, 0, 0)),
        ],
        out_specs=pl.BlockSpec((1, rows, w), lambda i: (i, 0, 0)),
        out_shape=jax.ShapeDtypeStruct((g, rows, w), F32),
        scratch_shapes=[pltpu.VMEM((rows, st // 2), F32)] * 6,
        compiler_params=pltpu.CompilerParams(dimension_semantics=("arbitrary",),
                                             vmem_limit_bytes=BIG_VMEM_LIMIT),
        name="s5_apply",
    )(u_cm, d_cm, m, zb, oc, aq)


def _na_bias_kernel(rpb_ref, o_ref):
    st = pl.program_id(0)
    h = pl.program_id(1)
    n_rel_rows = 2 * NA_ROWS - 1
    n_rel_cols = 2 * NA_COLS - 1
    w = lax.broadcasted_iota(jnp.int32, (GRID_W, LANES), 0)
    lane = lax.broadcasted_iota(jnp.int32, (GRID_W, LANES), 1)
    wk = lane & (GRID_W - 1)
    first = lane < GRID_W
    cs = jnp.clip(w - NA_COLS // 2, 0, GRID_W - NA_COLS)
    valid = (wk >= cs) & (wk < cs + NA_COLS)
    rel = wk - w + NA_COLS - 1
    for i in range(NA_ROWS * GRID_W // LANES):
        base0 = (h * n_rel_rows + st + 2 * i) * n_rel_cols
        base1 = base0 + n_rel_cols
        acc = jnp.full((GRID_W, LANES), NEG_INF, F32)
        for j in range(n_rel_cols):
            val = jnp.where(first, rpb_ref[base0 + j], rpb_ref[base1 + j]) * LOG2E
            acc = jnp.where(valid & (rel == j), val, acc)
        o_ref[0, 0, :, i * LANES:(i + 1) * LANES] = acc


def _na_bias(rpb):
    h = rpb.shape[0]
    pairs = h // 2
    return pl.pallas_call(
        _na_bias_kernel,
        grid=(NA_ROWS, h),
        in_specs=[pl.BlockSpec(memory_space=pltpu.SMEM)],
        out_specs=pl.BlockSpec((1, 1, GRID_W, NA_ROWS * GRID_W), lambda s, i: (s, i // 2, i % 2, 0)),
        out_shape=jax.ShapeDtypeStruct((NA_ROWS, pairs, 2 * GRID_W, NA_ROWS * GRID_W), F32),
        compiler_params=_params(2),
        name="na_bias",
    )(rpb.reshape(-1))


def _na_kernel(q_ref, k_ref, v_ref, b_ref, o_ref, s_scr, p_scr, *, rows):
    lane = lax.broadcasted_iota(jnp.int32, (GRID_W, LANES), 1)
    first = lane < NA_HEAD_DIM
    n_keys = NA_ROWS * GRID_W
    ones = jnp.ones((n_keys, LANES), BF16)

    def row_start(r):
        return jnp.clip(r - NA_ROWS // 2, 0, rows - NA_ROWS)

    def row_group(i, carry):
        base = i * NA_ROWS_PER_STEP
        for j in range(NA_ROWS_PER_STEP):
            r = base + j
            rs = row_start(r)
            q = q_ref[0, r]
            zero = jnp.zeros_like(q)
            qs = jnp.concatenate([jnp.where(first, q, zero), jnp.where(first, zero, q)], axis=0)
            kw = k_ref[0, pl.ds(rs, NA_ROWS)].reshape(n_keys, LANES)
            s_scr[j] = _dot_nt(qs, kw) + b_ref[rs - r + NA_ROWS - 1, 0]
        s = s_scr[...]
        p_scr[...] = jnp.exp2(s - jnp.max(s, axis=-1, keepdims=True)).astype(BF16)
        for j in range(NA_ROWS_PER_STEP):
            r = base + j
            vw = v_ref[0, pl.ds(row_start(r), NA_ROWS)].reshape(n_keys, LANES)
            on = _dot(p_scr[j], jnp.concatenate([vw, ones], axis=1))
            o = on[:, :LANES] / on[:, LANES:]
            o_ref[0, r] = jnp.where(first, o[:GRID_W], o[GRID_W:]).astype(o_ref.dtype)
        return carry

    lax.fori_loop(0, rows // NA_ROWS_PER_STEP, row_group, 0)


def _na(q, k, v, bias, bsz, rows):
    t, width = q.shape
    pairs = width // LANES
    shape4 = (bsz, rows, GRID_W, width)
    blk = pl.BlockSpec((1, rows, GRID_W, LANES), lambda b, i: (b, 0, 0, i))
    out = pl.pallas_call(
        functools.partial(_na_kernel, rows=rows),
        grid=(bsz, pairs),
        in_specs=[blk, blk, blk,
                  pl.BlockSpec((NA_ROWS, 1, 2 * GRID_W, NA_ROWS * GRID_W), lambda b, i: (0, i, 0, 0))],
        out_specs=blk,
        out_shape=jax.ShapeDtypeStruct(shape4, BF16),
        scratch_shapes=[pltpu.VMEM((NA_ROWS_PER_STEP, 2 * GRID_W, NA_ROWS * GRID_W), F32),
                        pltpu.VMEM((NA_ROWS_PER_STEP, 2 * GRID_W, NA_ROWS * GRID_W), BF16)],
        compiler_params=_params(2),
        name="na_attn",
    )(q.reshape(shape4), k.reshape(shape4), v.reshape(shape4), bias)
    return out.reshape(t, width)


def _out_even_kernel(ycm_ref, na_ref, x_ref, wglu_ref, bglu_ref, wo_ref, o_ref, t_scr):
    groups, n_chunks, _ = ycm_ref.shape
    half = groups * S5_GROUP
    for j in range(half):
        grp, chan = divmod(j, S5_GROUP)
        t_scr[pl.ds(j, n_chunks, stride=half), :] = ycm_ref[grp, :, chan * CHUNK:(chan + 1) * CHUNK]
    y = jnp.concatenate([t_scr[ch * half:(ch + 1) * half, :].T for ch in range(n_chunks)], axis=0)
    g = jax.nn.gelu(y)
    a = g * jax.nn.sigmoid(_dot(g.astype(BF16), wglu_ref[...]) + bglu_ref[...])
    mix = _dot(a.astype(BF16), wo_ref[:half, :]) + _dot(na_ref[...], wo_ref[half:, :])
    o_ref[...] = x_ref[...] + mix


def _out_even(y_cm, na, x, w_glu, b_glu, w_out):
    t, d = x.shape
    groups = y_cm.shape[0]
    half = groups * S5_GROUP
    tm = EVEN_TOKEN_TILE
    n_chunks = tm // CHUNK
    row = lambda w: pl.BlockSpec((tm, w), lambda i: (i, 0))
    full = lambda r, w: pl.BlockSpec((r, w), lambda i: (0, 0))
    return pl.pallas_call(
        _out_even_kernel,
        grid=(t // tm,),
        in_specs=[pl.BlockSpec((groups, n_chunks, S5_GROUP * CHUNK), lambda i: (0, i, 0)),
                  row(half), row(d), full(half, half), full(1, half), full(d, d)],
        out_specs=row(d),
        out_shape=jax.ShapeDtypeStruct((t, d), F32),
        scratch_shapes=[pltpu.VMEM((n_chunks * half, CHUNK), F32)],
        compiler_params=_params(1),
        name="out_even",
    )(y_cm, na, x, w_glu, b_glu, w_out)


def _t5_index_table():
    half = T5_BUCKETS // 2
    max_exact = half // 2
    rel = np.arange(3 * BLOCK)[None, :] - BLOCK - np.arange(BLOCK)[:, None]
    n = np.abs(rel)
    assert T5_MAX_DIST // max_exact == 16 and half - max_exact == 8 and max_exact == 8
    sq = np.maximum(n, 1).astype(np.int64) ** 2
    log2_sq = np.floor(np.log2(sq.astype(np.float64) + 0.5)).astype(np.int64)
    large = np.minimum(max_exact + log2_sq - 6, half - 1)
    bucket = np.where(rel > 0, half, 0) + np.where(n < max_exact, n, large)
    return np.where(n <= WINDOW, bucket, -1).astype(np.int32)


def _t5_bias_kernel(idx_ref, t5_ref, o_ref, *, heads):
    variant = pl.program_id(0)
    h = pl.program_id(1)
    idx = idx_ref[...]
    acc = jnp.full(idx.shape, NEG_INF, F32)
    for b in range(T5_BUCKETS):
        acc = jnp.where(idx == b, t5_ref[b * heads + h] * LOG2E, acc)
    col = lax.broadcasted_iota(jnp.int32, idx.shape, 1)
    absent = ((variant == 1) & (col < BLOCK)) | ((variant == 2) & (col >= 2 * BLOCK))
    o_ref[0, 0] = jnp.where(absent, NEG_INF, acc)


def _t5_bias(t5_table):
    heads = t5_table.shape[1]
    idx = jnp.asarray(_t5_index_table())
    per_pair = 2 * GQA_GROUP
    return pl.pallas_call(
        functools.partial(_t5_bias_kernel, heads=heads),
        grid=(3, heads),
        in_specs=[pl.BlockSpec((BLOCK, 3 * BLOCK), lambda e, h: (0, 0)),
                  pl.BlockSpec(memory_space=pltpu.SMEM)],
        out_specs=pl.BlockSpec(
            (1, 1, BLOCK, 3 * BLOCK),
            lambda e, h: (e, (h // per_pair) * GQA_GROUP + h % GQA_GROUP, (h % per_pair) // GQA_GROUP, 0)),
        out_shape=jax.ShapeDtypeStruct((3, heads // 2, 2 * BLOCK, 3 * BLOCK), F32),
        compiler_params=_params(2),
        name="t5_bias",
    )(idx, t5_table.reshape(-1))


def _gqa_head_order(heads):
    order = []
    for tt in range(heads // 2):
        pair, t = divmod(tt, GQA_GROUP)
        order += [2 * GQA_GROUP * pair + t, 2 * GQA_GROUP * pair + GQA_GROUP + t]
    return order


def _gqa_kernel(q_ref, kp_ref, kc_ref, kn_ref, vp_ref, vc_ref, vn_ref, bias_ref, sink_ref, o_ref, s_scr, p_scr,
                *, heads):
    k = jnp.concatenate([kp_ref[...], kc_ref[...], kn_ref[...]], axis=0)
    v = jnp.concatenate([vp_ref[...], vc_ref[...], vn_ref[...]], axis=0)
    lane = lax.broadcasted_iota(jnp.int32, (BLOCK, LANES), 1)
    first = lane < GQA_HEAD_DIM
    upper = lax.broadcasted_iota(jnp.int32, (2 * BLOCK, 1), 0) < BLOCK
    ones = jnp.ones((3 * BLOCK, LANES), BF16)
    order = _gqa_head_order(heads)
    tiles = heads // 2
    for tt in range(tiles):
        k2 = k[:, tt // GQA_GROUP * LANES:(tt // GQA_GROUP + 1) * LANES]
        tile = q_ref[:, tt * LANES:(tt + 1) * LANES]
        zero = jnp.zeros_like(tile)
        qs = jnp.concatenate([jnp.where(first, tile, zero), jnp.where(first, zero, tile)], axis=0)
        s_scr[tt] = _dot_nt(qs, k2) + bias_ref[0, tt]
    s = s_scr[...]
    m = jnp.max(s, axis=-1, keepdims=True)
    p_scr[...] = jnp.exp2(s - m).astype(BF16)
    for tt in range(tiles):
        pair = tt // GQA_GROUP
        v2 = jnp.concatenate([v[:, pair * LANES:(pair + 1) * LANES], ones], axis=1)
        on = _dot(p_scr[tt], v2)
        sk = jnp.where(upper, sink_ref[order[2 * tt]], sink_ref[order[2 * tt + 1]]) * LOG2E
        o = on[:, :LANES] / (on[:, LANES:] + jnp.exp2(sk - m[tt]))
        o_ref[:, tt * LANES:(tt + 1) * LANES] = jnp.where(first, o[:BLOCK], o[BLOCK:]).astype(o_ref.dtype)


def _gqa(q, k, v, bias, sink, bsz, nb):
    t, qw = q.shape
    kw = k.shape[1]
    heads = qw // GQA_HEAD_DIM
    cur = lambda w: pl.BlockSpec((BLOCK, w), lambda b, n: (b * nb + n, 0))
    prev = lambda w: pl.BlockSpec((BLOCK, w), lambda b, n: (b * nb + jnp.maximum(n - 1, 0), 0))
    nxt = lambda w: pl.BlockSpec((BLOCK, w), lambda b, n: (b * nb + jnp.minimum(n + 1, nb - 1), 0))
    assert nb >= 2, "a sequence shorter than two blocks has no separate first / last block variant"
    variant = lambda n: jnp.where(n == 0, 1, jnp.where(n == nb - 1, 2, 0))
    return pl.pallas_call(
        functools.partial(_gqa_kernel, heads=heads),
        grid=(bsz, nb),
        in_specs=[cur(qw), prev(kw), cur(kw), nxt(kw), prev(kw), cur(kw), nxt(kw),
                  pl.BlockSpec((1, heads // 2, 2 * BLOCK, 3 * BLOCK), lambda b, n: (variant(n), 0, 0, 0)),
                  pl.BlockSpec(memory_space=pltpu.SMEM)],
        out_specs=cur(qw),
        out_shape=jax.ShapeDtypeStruct((t, qw), BF16),
        scratch_shapes=[pltpu.VMEM((heads // 2, 2 * BLOCK, 3 * BLOCK), F32),
                        pltpu.VMEM((heads // 2, 2 * BLOCK, 3 * BLOCK), BF16)],
        compiler_params=_params(2),
        name="gqa_attn",
    )(q, k, k, k, v, v, v, bias, sink)


def _prepare(p):
    depth = p["norm_mix"].shape[0]
    row = lambda a: a.reshape(1, -1).astype(F32)
    prep = {"depth": depth, "layers": []}
    heads = p["gqa_sink"].shape[-1] if depth > 1 else 0
    if depth > 1:
        order = np.asarray(_gqa_head_order(heads))
        slot_cols = (order[:, None] * GQA_HEAD_DIM + np.arange(GQA_HEAD_DIM)[None, :]).reshape(-1)
        prep["t5_bias"] = _t5_bias(p["t5_table"].astype(F32))
    for layer in range(depth):
        i = layer // 2
        lp = {
            "ffn": [(row(p["norm_ffn"][layer, j]), p["w_ffn_gate"][layer, j].astype(BF16),
                     p["w_ffn_up"][layer, j].astype(BF16), p["w_ffn_down"][layer, j].astype(BF16))
                    for j in range(2)],
            "norm_mix": row(p["norm_mix"][layer]),
        }
        if layer % 2 == 0:
            lp["w_in"] = p["w_in_even"][i].astype(BF16)
            lp["tables"] = _s5_tables(p["s5_lam_re"][i].astype(F32), p["s5_lam_im"][i].astype(F32),
                                      p["s5_log_dt"][i].astype(F32), p["s5_b_re"][i].astype(F32),
                                      p["s5_b_im"][i].astype(F32), p["s5_c_re"][i].astype(F32),
                                      p["s5_c_im"][i].astype(F32))
            groups = p["s5_lam_re"].shape[2]
            lp["d_skip"] = jnp.repeat(p["s5_d"][i].astype(F32).reshape(groups, 1, S5_GROUP), CHUNK, axis=2)
            lp["w_glu"] = p["s5_w_glu"][i].astype(BF16)
            lp["b_glu"] = row(p["s5_b_glu"][i])
            lp["na_bias"] = _na_bias(p["na_rpb"][i].astype(F32))
            lp["w_out"] = p["w_out_even"][i].astype(BF16)
        else:
            w_in = p["w_in_odd"][i]
            nq = heads * GQA_HEAD_DIM
            lp["w_in"] = jnp.concatenate([w_in[:, slot_cols], w_in[:, nq:]], axis=1).astype(BF16)
            lp["sink"] = p["gqa_sink"][i].astype(F32)
            lp["w_out"] = p["w_out_odd"][i][slot_cols, :].astype(BF16)
        prep["layers"].append(lp)
    prep["norm_final"] = row(p["norm_final"])
    return prep


def _trunk(x, prep):
    bsz, seq, d = x.shape
    t = bsz * seq
    x = x.reshape(t, d).astype(F32)
    depth = prep["depth"]
    for layer, lp in enumerate(prep["layers"]):
        x = _ffn(x, *lp["ffn"][0], prep["norm_final"], False)
        if layer % 2 == 0:
            u_cm, q, k, v = _inproj_even(x, lp["norm_mix"], lp["w_in"], NA_HEAD_DIM ** -0.5 * LOG2E)
            y_cm = _s5_apply(u_cm, lp["d_skip"], *lp["tables"], seq // CHUNK, bsz)
            na = _na(q, k, v, lp["na_bias"], bsz, seq // GRID_W)
            x = _out_even(y_cm, na, x, lp["w_glu"], lp["b_glu"], lp["w_out"])
            mixed = None
        else:
            heads = lp["sink"].shape[0]
            nq = heads * GQA_HEAD_DIM
            nkv = nq // GQA_GROUP
            scale = GQA_HEAD_DIM ** -0.5 * LOG2E
            q, k, v = _inproj(x, lp["norm_mix"], lp["w_in"],
                              [(nq, BF16, scale), (nkv, BF16, 1.0), (nkv, BF16, 1.0)])
            mixed = (_gqa(q, k, v, prep["t5_bias"], lp["sink"], bsz, seq // BLOCK), lp["w_out"])
        x = _ffn(x, *lp["ffn"][1], prep["norm_final"], layer == depth - 1, mixed)
    return x.reshape(bsz, seq, d)


def kernel(x_prompt, x_sample, norm_ffn, w_ffn_gate, w_ffn_up, w_ffn_down, norm_mix, w_in_even, s5_lam_re, s5_lam_im, s5_log_dt, s5_b_re, s5_b_im, s5_c_re, s5_c_im, s5_d, s5_w_glu, s5_b_glu, na_rpb, w_out_even, w_in_odd, gqa_sink, w_out_odd, t5_table, norm_final):
    prep = _prepare(dict(
        norm_ffn=norm_ffn, w_ffn_gate=w_ffn_gate, w_ffn_up=w_ffn_up, w_ffn_down=w_ffn_down,
        norm_mix=norm_mix, w_in_even=w_in_even, s5_lam_re=s5_lam_re, s5_lam_im=s5_lam_im,
        s5_log_dt=s5_log_dt, s5_b_re=s5_b_re, s5_b_im=s5_b_im, s5_c_re=s5_c_re, s5_c_im=s5_c_im,
        s5_d=s5_d, s5_w_glu=s5_w_glu, s5_b_glu=s5_b_glu, na_rpb=na_rpb, w_out_even=w_out_even,
        w_in_odd=w_in_odd, gqa_sink=gqa_sink, w_out_odd=w_out_odd, t5_table=t5_table,
        norm_final=norm_final))
    return (_trunk(x_prompt, prep), _trunk(x_sample, prep))
```

```python
import functools
import math

import jax
import jax.numpy as jnp
import numpy as np
from jax import lax
from jax.experimental import pallas as pl
from jax.experimental.pallas import tpu as pltpu

F32 = jnp.float32
BF16 = jnp.bfloat16

S5_GROUP = 16
S5_STATE = 64
NA_HEAD_DIM = 64
NA_ROWS = 8
NA_COLS = 16
GRID_W = 64
GQA_HEAD_DIM = 64
GQA_GROUP = 4
WINDOW = 128
BLOCK = 128
T5_BUCKETS = 32
T5_MAX_DIST = 128
RMS_EPS = 1e-6
NEG_INF = -1e30
LOG2E = math.log2(math.e)

LANES = 128
CHUNK = LANES
EVEN_TOKEN_TILE = 8 * CHUNK
FFN_TOKEN_TILE = 1024
FFN_CHUNK = 512
NA_ROWS_PER_STEP = 16
VMEM_LIMIT = 48 * 1024 * 1024
BIG_VMEM_LIMIT = 60 * 1024 * 1024

_NT = (((1,), (1,)), ((), ()))
_HI = lax.Precision.HIGHEST


def _params(n_axes):
    return pltpu.CompilerParams(dimension_semantics=("arbitrary",) * n_axes,
                                vmem_limit_bytes=VMEM_LIMIT)


def _rms(x, g):
    return x * lax.rsqrt(jnp.mean(x * x, axis=-1, keepdims=True) + RMS_EPS) * g


def _dot(a, b):
    return jnp.dot(a, b, preferred_element_type=F32)


def _dot_nt(a, b, precision=None):
    return lax.dot_general(a, b, _NT, precision=precision, preferred_element_type=F32)


def _half_ffn(x, g_ref, wg_ref, wu_ref, wd_ref, bounds):
    hn = _rms(x, g_ref[...]).astype(BF16)
    acc = None
    for lo, hi in bounds:
        gate = _dot(hn, wg_ref[:, lo:hi])
        up = _dot(hn, wu_ref[:, lo:hi])
        h = (gate * jax.nn.sigmoid(gate) * up).astype(BF16)
        part = _dot(h, wd_ref[lo:hi, :])
        acc = part if acc is None else acc + part
    return x + 0.5 * acc


def _project(x, g_ref, w_ref, o_refs, scales):
    hn = _rms(x, g_ref[...]).astype(BF16)
    off = 0
    for o_ref, scale in zip(o_refs, scales):
        width = o_ref.shape[1]
        part = _dot(hn, w_ref[:, off:off + width])
        if scale != 1.0:
            part = part * scale
        o_ref[...] = part.astype(o_ref.dtype)
        off += width


def _ffn_kernel(x_ref, g_ref, wg_ref, wu_ref, wd_ref, gf_ref, o_ref, *, final_norm, bounds):
    y = _half_ffn(x_ref[...], g_ref, wg_ref, wu_ref, wd_ref, bounds)
    o_ref[...] = _rms(y, gf_ref[...]) if final_norm else y


def _ffn_inproj_kernel(x_ref, g_ref, wg_ref, wu_ref, wd_ref, g2_ref, win_ref, o_ref, *p_refs, bounds, scales):
    y = _half_ffn(x_ref[...], g_ref, wg_ref, wu_ref, wd_ref, bounds)
    o_ref[...] = y
    _project(y, g2_ref, win_ref, p_refs, scales)


def _proj_ffn_kernel(a_ref, wo_ref, x_ref, g_ref, wg_ref, wu_ref, wd_ref, gf_ref, o_ref, *, final_norm, bounds):
    x = x_ref[...] + _dot(a_ref[...], wo_ref[...])
    y = _half_ffn(x, g_ref, wg_ref, wu_ref, wd_ref, bounds)
    o_ref[...] = _rms(y, gf_ref[...]) if final_norm else y


def _ffn(x, g, wg, wu, wd, *, mixed=None, project=None, final_norm=None):
    t, d = x.shape
    f = wg.shape[1]
    tm = FFN_TOKEN_TILE
    bounds = tuple((lo, min(lo + FFN_CHUNK, f)) for lo in range(0, f, FFN_CHUNK))
    resident = lambda shape: pl.BlockSpec(shape, lambda i: (0, 0), pipeline_mode=pl.Buffered(1))
    row = lambda width: pl.BlockSpec((tm, width), lambda i: (i, 0))
    vec = pl.BlockSpec((1, d), lambda i: (0, 0))
    in_specs = [row(d), vec, resident((d, f)), resident((d, f)), resident((f, d))]
    args = (x, g, wg, wu, wd)
    out_specs = [row(d)]
    out_shape = [jax.ShapeDtypeStruct((t, d), F32)]
    final_gain = final_norm if final_norm is not None else g
    if project is not None:
        assert mixed is None and final_norm is None
        g2, w_in, splits = project
        in_specs += [vec, resident(w_in.shape)]
        args += (g2, w_in)
        out_specs += [row(width) for width, _, _ in splits]
        out_shape += [jax.ShapeDtypeStruct((t, width), dt) for width, dt, _ in splits]
        body = functools.partial(_ffn_inproj_kernel, bounds=bounds, scales=tuple(s for _, _, s in splits))
        name = "ffn_inproj"
    elif mixed is not None:
        a, w_out = mixed
        in_specs = [row(a.shape[1]), resident(w_out.shape)] + in_specs + [vec]
        args = (a, w_out) + args + (final_gain,)
        body = functools.partial(_proj_ffn_kernel, final_norm=final_norm is not None, bounds=bounds)
        name = "proj_ffn"
    else:
        in_specs += [vec]
        args += (final_gain,)
        body = functools.partial(_ffn_kernel, final_norm=final_norm is not None, bounds=bounds)
        name = "ffn"
    out = pl.pallas_call(
        body,
        grid=(t // tm,),
        in_specs=in_specs,
        out_specs=out_specs,
        out_shape=out_shape,
        compiler_params=pltpu.CompilerParams(dimension_semantics=("arbitrary",),
                                             vmem_limit_bytes=BIG_VMEM_LIMIT),
        name=name,
    )(*args)
    return out[0] if project is None else out


def _inproj_even_kernel(x_ref, g_ref, w_ref, ucm_ref, q_ref, k_ref, v_ref, t_scr, *, scale):
    tm = x_ref.shape[0]
    half = q_ref.shape[1]
    n_chunks = tm // CHUNK
    hn = _rms(x_ref[...], g_ref[...]).astype(BF16)
    u = _dot(hn, w_ref[:, :half])
    for ch in range(n_chunks):
        t_scr[ch * half:(ch + 1) * half, :] = u[ch * CHUNK:(ch + 1) * CHUNK, :].T
    q_ref[...] = (_dot(hn, w_ref[:, half:2 * half]) * scale).astype(q_ref.dtype)
    k_ref[...] = _dot(hn, w_ref[:, 2 * half:3 * half]).astype(k_ref.dtype)
    v_ref[...] = _dot(hn, w_ref[:, 3 * half:]).astype(v_ref.dtype)
    for j in range(half):
        grp, chan = divmod(j, S5_GROUP)
        ucm_ref[grp, :, chan * CHUNK:(chan + 1) * CHUNK] = t_scr[pl.ds(j, n_chunks, stride=half), :]


def _inproj_even(x, g, w, scale):
    t, d = x.shape
    half = w.shape[1] // 4
    groups = half // S5_GROUP
    tm = EVEN_TOKEN_TILE
    n_chunks = tm // CHUNK
    tok = lambda width: pl.BlockSpec((tm, width), lambda i: (i, 0))
    return pl.pallas_call(
        functools.partial(_inproj_even_kernel, scale=scale),
        grid=(t // tm,),
        in_specs=[tok(d), pl.BlockSpec((1, d), lambda i: (0, 0)), pl.BlockSpec(w.shape, lambda i: (0, 0))],
        out_specs=[pl.BlockSpec((groups, n_chunks, S5_GROUP * CHUNK), lambda i: (0, i, 0)),
                   tok(half), tok(half), tok(half)],
        out_shape=[jax.ShapeDtypeStruct((groups, t // CHUNK, S5_GROUP * CHUNK), F32)] +
                  [jax.ShapeDtypeStruct((t, half), BF16)] * 3,
        scratch_shapes=[pltpu.VMEM((n_chunks * half, CHUNK), F32)],
        compiler_params=_params(1),
        name="inproj_even",
    )(x, g, w)


def _cmul(ar, ai, br, bi):
    return ar * br - ai * bi, ar * bi + ai * br


def _discretise(lr, li, dt):
    mag = jnp.exp(lr * dt)
    ar = mag * jnp.cos(li * dt)
    ai = mag * jnp.sin(li * dt)
    den = lr * lr + li * li
    nr = ar - 1.0
    return ar, ai, (nr * lr + ai * li) / den, (ai * lr - nr * li) / den


def _powers(exponent, ar, ai, shape):
    pr = jnp.ones(shape, F32)
    pi = jnp.zeros(shape, F32)
    sr, si = ar, ai
    for j in range((2 * CHUNK).bit_length() - 1):
        bit = ((exponent >> j) & 1) == 1
        mr, mi = _cmul(pr, pi, sr, si)
        pr = jnp.where(bit, mr, pr)
        pi = jnp.where(bit, mi, pi)
        sr, si = _cmul(sr, si, sr, si)
    keep = exponent >= 0
    return jnp.where(keep, pr, 0.0), jnp.where(keep, pi, 0.0)


def _s5_table_kernel(lre_ref, lim_ref, ldt_ref, lre_col_ref, lim_col_ref, btr_ref, bti_ref, cr_ref, ci_ref,
                     ctr_ref, cti_ref, m_ref, zb_ref, oc_ref, aq_ref, k2_scr):
    q = CHUNK
    p = S5_STATE
    c = S5_GROUP
    lane2 = lax.broadcasted_iota(jnp.int32, (1, 2 * q), 1)
    lane1 = lax.broadcasted_iota(jnp.int32, (1, q), 1)
    row1 = lax.broadcasted_iota(jnp.int32, (q, 1), 0)

    kappa = None
    for d in range(2):
        dt = jnp.exp(ldt_ref[d, 0])
        ar, ai, zr, zi = _discretise(lre_ref[d, 0], lim_ref[d, 0], dt)
        acr, aci, _, _ = _discretise(lre_col_ref[d, 0], lim_col_ref[d, 0], dt)
        btr = btr_ref[d, 0]
        bti = bti_ref[d, 0]
        bbr = zr * btr - zi * bti
        bbi = zr * bti + zi * btr
        cre = cr_ref[d, 0]
        cim = ci_ref[d, 0]

        rep = lambda a: jnp.concatenate([jnp.broadcast_to(a[i:i + 1], (c, p)) for i in range(c)], axis=0)
        til = lambda a: jnp.concatenate([a] * c, axis=0)
        wr, wi = _cmul(rep(bbr), rep(bbi), til(cre), til(cim))
        lag = (lane2 - q) if d == 0 else (q - lane2)
        pr, pi = _powers(lag, acr, aci, (p, 2 * q))
        part = jnp.dot(wr, pr, precision=_HI, preferred_element_type=F32) - \
            jnp.dot(wi, pi, precision=_HI, preferred_element_type=F32)
        kappa = part if kappa is None else kappa + part

        zpr, zpi = _powers((q - 1 - row1) if d == 0 else row1, ar, ai, (q, p))
        for i in range(c):
            zbr, zbi = _cmul(bbr[i:i + 1], bbi[i:i + 1], zpr, zpi)
            zb_ref[0, i * q:(i + 1) * q, d * p:(d + 1) * p] = zbr.astype(BF16)
            zb_ref[0, i * q:(i + 1) * q, 2 * p + d * p:2 * p + (d + 1) * p] = zbi.astype(BF16)

        opr, opi = _powers((lane1 + 1) if d == 0 else (q - lane1), acr, aci, (p, q))
        ctr = ctr_ref[d, 0]
        cti = cti_ref[d, 0]
        for i in range(c):
            e_re, e_im = _cmul(ctr[:, i:i + 1], cti[:, i:i + 1], opr, opi)
            oc_ref[0, d * p:(d + 1) * p, i * q:(i + 1) * q] = e_re.astype(BF16)
            oc_ref[0, 2 * p + d * p:2 * p + (d + 1) * p, i * q:(i + 1) * q] = (-e_im).astype(BF16)

        aqr, aqi = _powers(jnp.full((1, 1), q, jnp.int32), ar, ai, (1, p))
        aq_ref[0, :, d * p:(d + 1) * p] = aqr
        aq_ref[0, :, 2 * p + d * p:2 * p + (d + 1) * p] = aqi

    k2_scr[...] = kappa

    def fill(i, carry):
        for j in range(c):
            row = k2_scr[pl.ds(i * c + j, 1), :]
            toe = pltpu.roll(jnp.broadcast_to(row, (q, 2 * q)), q, axis=1, stride=1, stride_axis=0)
            m_ref[0, pl.ds(pl.multiple_of(i * q, q), q), j * q:(j + 1) * q] = toe[:, :q].astype(BF16)
        return carry

    lax.fori_loop(0, c, fill, 0)


def _s5_tables(lam_re, lam_im, log_dt, b_re, b_im, c_re, c_im):
    g = lam_re.shape[1]
    p, c, q = S5_STATE, S5_GROUP, CHUNK
    swap = lambda a: jnp.swapaxes(a, -1, -2)
    spec4 = lambda r, w: pl.BlockSpec((2, 1, r, w), lambda i: (0, i, 0, 0))
    return pl.pallas_call(
        _s5_table_kernel,
        grid=(g,),
        in_specs=[spec4(1, p), spec4(1, p), spec4(1, 1), spec4(p, 1), spec4(p, 1),
                  spec4(c, p), spec4(c, p), spec4(c, p), spec4(c, p), spec4(p, c), spec4(p, c)],
        out_specs=[
            pl.BlockSpec((1, q * c, q * c), lambda i: (i, 0, 0)),
            pl.BlockSpec((1, q * c, 4 * p), lambda i: (i, 0, 0)),
            pl.BlockSpec((1, 4 * p, q * c), lambda i: (i, 0, 0)),
            pl.BlockSpec((1, 1, 4 * p), lambda i: (i, 0, 0)),
        ],
        out_shape=[
            jax.ShapeDtypeStruct((g, q * c, q * c), BF16),
            jax.ShapeDtypeStruct((g, q * c, 4 * p), BF16),
            jax.ShapeDtypeStruct((g, 4 * p, q * c), BF16),
            jax.ShapeDtypeStruct((g, 1, 4 * p), F32),
        ],
        scratch_shapes=[pltpu.VMEM((c * c, 2 * q), F32)],
        compiler_params=_params(1),
        name="s5_tables",
    )(lam_re.reshape(2, g, 1, p), lam_im.reshape(2, g, 1, p), log_dt.reshape(2, g, 1, 1),
      lam_re.reshape(2, g, p, 1), lam_im.reshape(2, g, p, 1),
      swap(b_re), swap(b_im), c_re, c_im, swap(c_re), swap(c_im))


def _s5_kernel(u_ref, d_ref, m_ref, zb_ref, oc_ref, aq_ref, y_ref,
               zr_scr, zi_scr, fr_scr, fi_scr, br_scr, bi_scr, *, nc, bsz):
    half = 2 * S5_STATE
    uf = u_ref[0]
    u = uf.astype(BF16)
    z = _dot(u, zb_ref[0])
    zr_scr[...] = z[:, :half]
    zi_scr[...] = z[:, half:]
    aq = aq_ref[0]
    ar, ai = aq[:, :half], aq[:, half:]
    zero = jnp.zeros((bsz, half), F32)

    def step(j, carry):
        fr, fi, br, bi = carry
        rows_f = pl.ds(j, bsz, stride=nc)
        rows_b = pl.ds(nc - 1 - j, bsz, stride=nc)
        fr_scr[rows_f, :] = fr
        fi_scr[rows_f, :] = fi
        br_scr[rows_b, :] = br
        bi_scr[rows_b, :] = bi
        nfr, nfi = _cmul(ar, ai, fr, fi)
        nbr, nbi = _cmul(ar, ai, br, bi)
        return (nfr + zr_scr[rows_f, :], nfi + zi_scr[rows_f, :],
                nbr + zr_scr[rows_b, :], nbi + zi_scr[rows_b, :])

    lax.fori_loop(0, nc, step, (zero, zero, zero, zero))
    is_fwd = lax.broadcasted_iota(jnp.int32, (1, half), 1) < S5_STATE
    s = jnp.concatenate([jnp.where(is_fwd, fr_scr[...], br_scr[...]),
                         jnp.where(is_fwd, fi_scr[...], bi_scr[...])], axis=1).astype(BF16)
    y_ref[0] = d_ref[0] * uf + _dot(u, m_ref[0]) + _dot(s, oc_ref[0])


def _s5_apply(u_cm, d_cm, m, zb, oc, aq, nc, bsz):
    g, rows, w = u_cm.shape
    st = zb.shape[2]
    return pl.pallas_call(
        functools.partial(_s5_kernel, nc=nc, bsz=bsz),
        grid=(g,),
        in_specs=[
            pl.BlockSpec((1, rows, w), lambda i: (i, 0, 0)),
            pl.BlockSpec((1, 1, w), lambda i: (i, 0, 0)),
            pl.BlockSpec((1, w, w), lambda i: (i, 0, 0)),
            pl.BlockSpec((1, w, st), lambda i: (i, 0, 0)),
            pl.BlockSpec((1, st, w), lambda i: (i, 0, 0)),
            pl.BlockSpec((1, 1, st), lambda i: (i, 0, 0)),
        ],
        out_specs=pl.BlockSpec((1, rows, w), lambda i: (i, 0, 0)),
        out_shape=jax.ShapeDtypeStruct((g, rows, w), F32),
        scratch_shapes=[pltpu.VMEM((rows, st // 2), F32)] * 6,
        compiler_params=pltpu.CompilerParams(dimension_semantics=("arbitrary",),
                                             vmem_limit_bytes=BIG_VMEM_LIMIT),
        name="s5_apply",
    )(u_cm, d_cm, m, zb, oc, aq)


def _na_bias_kernel(rpb_ref, o_ref):
    st = pl.program_id(0)
    h = pl.program_id(1)
    n_rel_rows = 2 * NA_ROWS - 1
    n_rel_cols = 2 * NA_COLS - 1
    w = lax.broadcasted_iota(jnp.int32, (GRID_W, LANES), 0)
    lane = lax.broadcasted_iota(jnp.int32, (GRID_W, LANES), 1)
    wk = lane & (GRID_W - 1)
    first = lane < GRID_W
    cs = jnp.clip(w - NA_COLS // 2, 0, GRID_W - NA_COLS)
    valid = (wk >= cs) & (wk < cs + NA_COLS)
    rel = wk - w + NA_COLS - 1
    for i in range(NA_ROWS * GRID_W // LANES):
        base0 = (h * n_rel_rows + st + 2 * i) * n_rel_cols
        base1 = base0 + n_rel_cols
        acc = jnp.full((GRID_W, LANES), NEG_INF, F32)
        for j in range(n_rel_cols):
            val = jnp.where(first, rpb_ref[base0 + j], rpb_ref[base1 + j]) * LOG2E
            acc = jnp.where(valid & (rel == j), val, acc)
        o_ref[0, 0, :, i * LANES:(i + 1) * LANES] = acc


def _na_bias(rpb):
    h = rpb.shape[0]
    pairs = h // 2
    return pl.pallas_call(
        _na_bias_kernel,
        grid=(NA_ROWS, h),
        in_specs=[pl.BlockSpec(memory_space=pltpu.SMEM)],
        out_specs=pl.BlockSpec((1, 1, GRID_W, NA_ROWS * GRID_W), lambda s, i: (s, i // 2, i % 2, 0)),
        out_shape=jax.ShapeDtypeStruct((NA_ROWS, pairs, 2 * GRID_W, NA_ROWS * GRID_W), F32),
        compiler_params=_params(2),
        name="na_bias",
    )(rpb.reshape(-1))


def _na_kernel(q_ref, k_ref, v_ref, b_ref, o_ref, s_scr, p_scr, *, rows):
    lane = lax.broadcasted_iota(jnp.int32, (GRID_W, LANES), 1)
    first = lane < NA_HEAD_DIM
    n_keys = NA_ROWS * GRID_W
    ones = jnp.ones((n_keys, LANES), BF16)

    def row_start(r):
        return jnp.clip(r - NA_ROWS // 2, 0, rows - NA_ROWS)

    def row_group(i, carry):
        base = i * NA_ROWS_PER_STEP
        for j in range(NA_ROWS_PER_STEP):
            r = base + j
            rs = row_start(r)
            q = q_ref[0, r]
            zero = jnp.zeros_like(q)
            qs = jnp.concatenate([jnp.where(first, q, zero), jnp.where(first, zero, q)], axis=0)
            kw = k_ref[0, pl.ds(rs, NA_ROWS)].reshape(n_keys, LANES)
            s_scr[j] = _dot_nt(qs, kw) + b_ref[rs - r + NA_ROWS - 1, 0]
        s = s_scr[...]
        p_scr[...] = jnp.exp2(s - jnp.max(s, axis=-1, keepdims=True)).astype(BF16)
        for j in range(NA_ROWS_PER_STEP):
            r = base + j
            vw = v_ref[0, pl.ds(row_start(r), NA_ROWS)].reshape(n_keys, LANES)
            on = _dot(p_scr[j], jnp.concatenate([vw, ones], axis=1))
            o = on[:, :LANES] / on[:, LANES:]
            o_ref[0, r] = jnp.where(first, o[:GRID_W], o[GRID_W:]).astype(o_ref.dtype)
        return carry

    lax.fori_loop(0, rows // NA_ROWS_PER_STEP, row_group, 0)


def _na(q, k, v, bias, bsz, rows):
    t, width = q.shape
    pairs = width // LANES
    shape4 = (bsz, rows, GRID_W, width)
    blk = pl.BlockSpec((1, rows, GRID_W, LANES), lambda b, i: (b, 0, 0, i))
    out = pl.pallas_call(
        functools.partial(_na_kernel, rows=rows),
        grid=(bsz, pairs),
        in_specs=[blk, blk, blk,
                  pl.BlockSpec((NA_ROWS, 1, 2 * GRID_W, NA_ROWS * GRID_W), lambda b, i: (0, i, 0, 0))],
        out_specs=blk,
        out_shape=jax.ShapeDtypeStruct(shape4, BF16),
        scratch_shapes=[pltpu.VMEM((NA_ROWS_PER_STEP, 2 * GRID_W, NA_ROWS * GRID_W), F32),
                        pltpu.VMEM((NA_ROWS_PER_STEP, 2 * GRID_W, NA_ROWS * GRID_W), BF16)],
        compiler_params=_params(2),
        name="na_attn",
    )(q.reshape(shape4), k.reshape(shape4), v.reshape(shape4), bias)
    return out.reshape(t, width)


def _out_even_kernel(ycm_ref, na_ref, x_ref, wglu_ref, bglu_ref, wo_ref, o_ref, t_scr):
    groups, n_chunks, _ = ycm_ref.shape
    half = groups * S5_GROUP
    for j in range(half):
        grp, chan = divmod(j, S5_GROUP)
        t_scr[pl.ds(j, n_chunks, stride=half), :] = ycm_ref[grp, :, chan * CHUNK:(chan + 1) * CHUNK]
    y = jnp.concatenate([t_scr[ch * half:(ch + 1) * half, :].T for ch in range(n_chunks)], axis=0)
    g = jax.nn.gelu(y)
    a = g * jax.nn.sigmoid(_dot(g.astype(BF16), wglu_ref[...]) + bglu_ref[...])
    mix = _dot(a.astype(BF16), wo_ref[:half, :]) + _dot(na_ref[...], wo_ref[half:, :])
    o_ref[...] = x_ref[...] + mix


def _out_even(y_cm, na, x, w_glu, b_glu, w_out):
    t, d = x.shape
    groups = y_cm.shape[0]
    half = groups * S5_GROUP
    tm = EVEN_TOKEN_TILE
    n_chunks = tm // CHUNK
    row = lambda w: pl.BlockSpec((tm, w), lambda i: (i, 0))
    full = lambda r, w: pl.BlockSpec((r, w), lambda i: (0, 0))
    return pl.pallas_call(
        _out_even_kernel,
        grid=(t // tm,),
        in_specs=[pl.BlockSpec((groups, n_chunks, S5_GROUP * CHUNK), lambda i: (0, i, 0)),
                  row(half), row(d), full(half, half), full(1, half), full(d, d)],
        out_specs=row(d),
        out_shape=jax.ShapeDtypeStruct((t, d), F32),
        scratch_shapes=[pltpu.VMEM((n_chunks * half, CHUNK), F32)],
        compiler_params=_params(1),
        name="out_even",
    )(y_cm, na, x, w_glu, b_glu, w_out)


def _t5_index_table():
    half = T5_BUCKETS // 2
    max_exact = half // 2
    rel = np.arange(3 * BLOCK)[None, :] - BLOCK - np.arange(BLOCK)[:, None]
    n = np.abs(rel)
    assert T5_MAX_DIST // max_exact == 16 and half - max_exact == 8 and max_exact == 8
    sq = np.maximum(n, 1).astype(np.int64) ** 2
    log2_sq = np.floor(np.log2(sq.astype(np.float64) + 0.5)).astype(np.int64)
    large = np.minimum(max_exact + log2_sq - 6, half - 1)
    bucket = np.where(rel > 0, half, 0) + np.where(n < max_exact, n, large)
    return np.where(n <= WINDOW, bucket, -1).astype(np.int32)


def _t5_bias_kernel(idx_ref, t5_ref, o_ref, *, heads):
    variant = pl.program_id(0)
    h = pl.program_id(1)
    idx = idx_ref[...]
    acc = jnp.full(idx.shape, NEG_INF, F32)
    for b in range(T5_BUCKETS):
        acc = jnp.where(idx == b, t5_ref[b * heads + h] * LOG2E, acc)
    col = lax.broadcasted_iota(jnp.int32, idx.shape, 1)
    absent = ((variant == 1) & (col < BLOCK)) | ((variant == 2) & (col >= 2 * BLOCK))
    o_ref[0, 0] = jnp.where(absent, NEG_INF, acc)


def _t5_bias(t5_table):
    heads = t5_table.shape[1]
    idx = jnp.asarray(_t5_index_table())
    per_pair = 2 * GQA_GROUP
    return pl.pallas_call(
        functools.partial(_t5_bias_kernel, heads=heads),
        grid=(3, heads),
        in_specs=[pl.BlockSpec((BLOCK, 3 * BLOCK), lambda e, h: (0, 0)),
                  pl.BlockSpec(memory_space=pltpu.SMEM)],
        out_specs=pl.BlockSpec(
            (1, 1, BLOCK, 3 * BLOCK),
            lambda e, h: (e, (h // per_pair) * GQA_GROUP + h % GQA_GROUP, (h % per_pair) // GQA_GROUP, 0)),
        out_shape=jax.ShapeDtypeStruct((3, heads // 2, 2 * BLOCK, 3 * BLOCK), F32),
        compiler_params=_params(2),
        name="t5_bias",
    )(idx, t5_table.reshape(-1))


def _gqa_head_order(heads):
    order = []
    for tt in range(heads // 2):
        pair, t = divmod(tt, GQA_GROUP)
        order += [2 * GQA_GROUP * pair + t, 2 * GQA_GROUP * pair + GQA_GROUP + t]
    return order


def _gqa_kernel(q_ref, kp_ref, kc_ref, kn_ref, vp_ref, vc_ref, vn_ref, bias_a_ref, bias_b_ref, sink_ref, o_ref,
                s_scr, p_scr, *, heads):
    k_all = jnp.concatenate([kp_ref[...], kc_ref[...], kn_ref[...]], axis=0)
    v_all = jnp.concatenate([vp_ref[...], vc_ref[...], vn_ref[...]], axis=0)
    lane = lax.broadcasted_iota(jnp.int32, (BLOCK, LANES), 1)
    first = lane < GQA_HEAD_DIM
    upper = lax.broadcasted_iota(jnp.int32, (2 * BLOCK, 1), 0) < BLOCK
    ones = jnp.ones((3 * BLOCK, LANES), BF16)
    order = _gqa_head_order(heads)
    tiles = heads // 2
    for blk, bias_ref in enumerate((bias_a_ref, bias_b_ref)):
        rows = slice(blk * BLOCK, (blk + 1) * BLOCK)
        k = k_all[blk * BLOCK:(blk + 3) * BLOCK]
        v = v_all[blk * BLOCK:(blk + 3) * BLOCK]
        for tt in range(tiles):
            k2 = k[:, tt // GQA_GROUP * LANES:(tt // GQA_GROUP + 1) * LANES]
            tile = q_ref[rows, tt * LANES:(tt + 1) * LANES]
            zero = jnp.zeros_like(tile)
            qs = jnp.concatenate([jnp.where(first, tile, zero), jnp.where(first, zero, tile)], axis=0)
            s_scr[blk, tt] = _dot_nt(qs, k2) + bias_ref[0, tt]
        s = s_scr[blk]
        m = jnp.max(s, axis=-1, keepdims=True)
        p_scr[blk] = jnp.exp2(s - m).astype(BF16)
        for tt in range(tiles):
            pair = tt // GQA_GROUP
            v2 = jnp.concatenate([v[:, pair * LANES:(pair + 1) * LANES], ones], axis=1)
            on = _dot(p_scr[blk, tt], v2)
            sk = jnp.where(upper, sink_ref[order[2 * tt]], sink_ref[order[2 * tt + 1]]) * LOG2E
            o = on[:, :LANES] / (on[:, LANES:] + jnp.exp2(sk - m[tt]))
            o_ref[rows, tt * LANES:(tt + 1) * LANES] = \
                jnp.where(first, o[:BLOCK], o[BLOCK:]).astype(o_ref.dtype)


def _gqa(q, k, v, bias, sink, bsz, nb):
    t, qw = q.shape
    kw = k.shape[1]
    heads = qw // GQA_HEAD_DIM
    assert nb % 2 == 0, "query blocks are processed in pairs"
    steps = nb // 2
    cur = lambda w: pl.BlockSpec((2 * BLOCK, w), lambda b, j: (b * steps + j, 0))
    prev = lambda w: pl.BlockSpec((BLOCK, w), lambda b, j: (b * nb + jnp.maximum(2 * j - 1, 0), 0))
    nxt = lambda w: pl.BlockSpec((BLOCK, w), lambda b, j: (b * nb + jnp.minimum(2 * j + 2, nb - 1), 0))
    tile_bias = lambda variant: pl.BlockSpec((1, heads // 2, 2 * BLOCK, 3 * BLOCK),
                                             lambda b, j: (variant(j), 0, 0, 0))
    return pl.pallas_call(
        functools.partial(_gqa_kernel, heads=heads),
        grid=(bsz, steps),
        in_specs=[cur(qw), prev(kw), cur(kw), nxt(kw), prev(kw), cur(kw), nxt(kw),
                  tile_bias(lambda j: jnp.where(j == 0, 1, 0)),
                  tile_bias(lambda j: jnp.where(j == steps - 1, 2, 0)),
                  pl.BlockSpec(memory_space=pltpu.SMEM)],
        out_specs=cur(qw),
        out_shape=jax.ShapeDtypeStruct((t, qw), BF16),
        scratch_shapes=[pltpu.VMEM((2, heads // 2, 2 * BLOCK, 3 * BLOCK), F32),
                        pltpu.VMEM((2, heads // 2, 2 * BLOCK, 3 * BLOCK), BF16)],
        compiler_params=_params(2),
        name="gqa_attn",
    )(q, k, k, k, v, v, v, bias, bias, sink)


def _prepare(p):
    depth = p["norm_mix"].shape[0]
    row = lambda a: a.reshape(1, -1).astype(F32)
    prep = {"depth": depth, "layers": []}
    heads = p["gqa_sink"].shape[-1] if depth > 1 else 0
    if depth > 1:
        order = np.asarray(_gqa_head_order(heads))
        slot_cols = (order[:, None] * GQA_HEAD_DIM + np.arange(GQA_HEAD_DIM)[None, :]).reshape(-1)
        prep["t5_bias"] = _t5_bias(p["t5_table"].astype(F32))
    for layer in range(depth):
        i = layer // 2
        lp = {
            "ffn": [(row(p["norm_ffn"][layer, j]), p["w_ffn_gate"][layer, j].astype(BF16),
                     p["w_ffn_up"][layer, j].astype(BF16), p["w_ffn_down"][layer, j].astype(BF16))
                    for j in range(2)],
            "norm_mix": row(p["norm_mix"][layer]),
        }
        if layer % 2 == 0:
            lp["w_in"] = p["w_in_even"][i].astype(BF16)
            lp["tables"] = _s5_tables(p["s5_lam_re"][i].astype(F32), p["s5_lam_im"][i].astype(F32),
                                      p["s5_log_dt"][i].astype(F32), p["s5_b_re"][i].astype(F32),
                                      p["s5_b_im"][i].astype(F32), p["s5_c_re"][i].astype(F32),
                                      p["s5_c_im"][i].astype(F32))
            groups = p["s5_lam_re"].shape[2]
            lp["d_skip"] = jnp.repeat(p["s5_d"][i].astype(F32).reshape(groups, 1, S5_GROUP), CHUNK, axis=2)
            lp["w_glu"] = p["s5_w_glu"][i].astype(BF16)
            lp["b_glu"] = row(p["s5_b_glu"][i])
            lp["na_bias"] = _na_bias(p["na_rpb"][i].astype(F32))
            lp["w_out"] = p["w_out_even"][i].astype(BF16)
        else:
            w_in = p["w_in_odd"][i]
            nq = heads * GQA_HEAD_DIM
            lp["w_in"] = jnp.concatenate([w_in[:, slot_cols], w_in[:, nq:]], axis=1).astype(BF16)
            lp["sink"] = p["gqa_sink"][i].astype(F32)
            lp["w_out"] = p["w_out_odd"][i][slot_cols, :].astype(BF16)
        prep["layers"].append(lp)
    prep["norm_final"] = row(p["norm_final"])
    return prep


def _trunk(x, prep):
    bsz, seq, d = x.shape
    t = bsz * seq
    x = x.reshape(t, d).astype(F32)
    depth = prep["depth"]
    for layer, lp in enumerate(prep["layers"]):
        if layer % 2 == 0:
            x = _ffn(x, *lp["ffn"][0])
            u_cm, q, k, v = _inproj_even(x, lp["norm_mix"], lp["w_in"], NA_HEAD_DIM ** -0.5 * LOG2E)
            y_cm = _s5_apply(u_cm, lp["d_skip"], *lp["tables"], seq // CHUNK, bsz)
            na = _na(q, k, v, lp["na_bias"], bsz, seq // GRID_W)
            x = _out_even(y_cm, na, x, lp["w_glu"], lp["b_glu"], lp["w_out"])
            mixed = None
        else:
            heads = lp["sink"].shape[0]
            nq = heads * GQA_HEAD_DIM
            nkv = nq // GQA_GROUP
            scale = GQA_HEAD_DIM ** -0.5 * LOG2E
            x, q, k, v = _ffn(x, *lp["ffn"][0], project=(
                lp["norm_mix"], lp["w_in"], [(nq, BF16, scale), (nkv, BF16, 1.0), (nkv, BF16, 1.0)]))
            mixed = (_gqa(q, k, v, prep["t5_bias"], lp["sink"], bsz, seq // BLOCK), lp["w_out"])
        x = _ffn(x, *lp["ffn"][1], mixed=mixed, final_norm=prep["norm_final"] if layer == depth - 1 else None)
    return x.reshape(bsz, seq, d)


def kernel(x_prompt, x_sample, norm_ffn, w_ffn_gate, w_ffn_up, w_ffn_down, norm_mix, w_in_even, s5_lam_re, s5_lam_im, s5_log_dt, s5_b_re, s5_b_im, s5_c_re, s5_c_im, s5_d, s5_w_glu, s5_b_glu, na_rpb, w_out_even, w_in_odd, gqa_sink, w_out_odd, t5_table, norm_final):
    prep = _prepare(dict(
        norm_ffn=norm_ffn, w_ffn_gate=w_ffn_gate, w_ffn_up=w_ffn_up, w_ffn_down=w_ffn_down,
        norm_mix=norm_mix, w_in_even=w_in_even, s5_lam_re=s5_lam_re, s5_lam_im=s5_lam_im,
        s5_log_dt=s5_log_dt, s5_b_re=s5_b_re, s5_b_im=s5_b_im, s5_c_re=s5_c_re, s5_c_im=s5_c_im,
        s5_d=s5_d, s5_w_glu=s5_w_glu, s5_b_glu=s5_b_glu, na_rpb=na_rpb, w_out_even=w_out_even,
        w_in_odd=w_in_odd, gqa_sink=gqa_sink, w_out_odd=w_out_odd, t5_table=t5_table,
        norm_final=norm_final))
    return (_trunk(x_prompt, prep), _trunk(x_sample, prep))
```

```python
import functools
import math

import jax
import jax.numpy as jnp
import numpy as np
from jax import lax
from jax.experimental import pallas as pl
from jax.experimental.pallas import tpu as pltpu

F32 = jnp.float32
BF16 = jnp.bfloat16

S5_GROUP = 16
S5_STATE = 64
NA_HEAD_DIM = 64
NA_ROWS = 8
NA_COLS = 16
GRID_W = 64
GQA_HEAD_DIM = 64
GQA_GROUP = 4
WINDOW = 128
BLOCK = 128
T5_BUCKETS = 32
T5_MAX_DIST = 128
RMS_EPS = 1e-6
NEG_INF = -1e30
LOG2E = math.log2(math.e)

LANES = 128
CHUNK = LANES
EVEN_TOKEN_TILE = 8 * CHUNK
FFN_TOKEN_TILE = 1024
FFN_CHUNK = 512
FFN_ROW_SPLITS = 2
NA_ROWS_PER_STEP = 16
VMEM_LIMIT = 48 * 1024 * 1024
BIG_VMEM_LIMIT = 60 * 1024 * 1024

_NT = (((1,), (1,)), ((), ()))
_HI = lax.Precision.HIGHEST


def _params(n_axes):
    return pltpu.CompilerParams(dimension_semantics=("arbitrary",) * n_axes,
                                vmem_limit_bytes=VMEM_LIMIT)


def _rms(x, g):
    return x * lax.rsqrt(jnp.mean(x * x, axis=-1, keepdims=True) + RMS_EPS) * g


def _dot(a, b):
    return jnp.dot(a, b, preferred_element_type=F32)


def _dot_nt(a, b, precision=None):
    return lax.dot_general(a, b, _NT, precision=precision, preferred_element_type=F32)


def _half_ffn(x, g_ref, wg_ref, wu_ref, wd_ref, bounds):
    outs = []
    for xs in jnp.split(x, FFN_ROW_SPLITS, axis=0):
        hn = _rms(xs, g_ref[...]).astype(BF16)
        acc = None
        for lo, hi in bounds:
            gate = _dot(hn, wg_ref[:, lo:hi])
            up = _dot(hn, wu_ref[:, lo:hi])
            h = (gate * jax.nn.sigmoid(gate) * up).astype(BF16)
            part = _dot(h, wd_ref[lo:hi, :])
            acc = part if acc is None else acc + part
        outs.append(xs + 0.5 * acc)
    return jnp.concatenate(outs, axis=0)


def _project(x, g_ref, w_ref, o_refs, scales):
    hn = _rms(x, g_ref[...]).astype(BF16)
    off = 0
    for o_ref, scale in zip(o_refs, scales):
        width = o_ref.shape[1]
        part = _dot(hn, w_ref[:, off:off + width])
        if scale != 1.0:
            part = part * scale
        o_ref[...] = part.astype(o_ref.dtype)
        off += width


def _ffn_kernel(x_ref, g_ref, wg_ref, wu_ref, wd_ref, gf_ref, o_ref, *, final_norm, bounds):
    y = _half_ffn(x_ref[...], g_ref, wg_ref, wu_ref, wd_ref, bounds)
    o_ref[...] = _rms(y, gf_ref[...]) if final_norm else y


def _ffn_inproj_kernel(x_ref, g_ref, wg_ref, wu_ref, wd_ref, g2_ref, win_ref, o_ref, *p_refs, bounds, scales):
    y = _half_ffn(x_ref[...], g_ref, wg_ref, wu_ref, wd_ref, bounds)
    o_ref[...] = y
    _project(y, g2_ref, win_ref, p_refs, scales)


def _proj_ffn_kernel(a_ref, wo_ref, x_ref, g_ref, wg_ref, wu_ref, wd_ref, gf_ref, o_ref, *, final_norm, bounds):
    x = x_ref[...] + _dot(a_ref[...], wo_ref[...])
    y = _half_ffn(x, g_ref, wg_ref, wu_ref, wd_ref, bounds)
    o_ref[...] = _rms(y, gf_ref[...]) if final_norm else y


def _ffn(x, g, wg, wu, wd, *, mixed=None, project=None, final_norm=None):
    t, d = x.shape
    f = wg.shape[1]
    tm = FFN_TOKEN_TILE
    bounds = tuple((lo, min(lo + FFN_CHUNK, f)) for lo in range(0, f, FFN_CHUNK))
    resident = lambda shape: pl.BlockSpec(shape, lambda i: (0, 0), pipeline_mode=pl.Buffered(1))
    row = lambda width: pl.BlockSpec((tm, width), lambda i: (i, 0))
    vec = pl.BlockSpec((1, d), lambda i: (0, 0))
    in_specs = [row(d), vec, resident((d, f)), resident((d, f)), resident((f, d))]
    args = (x, g, wg, wu, wd)
    out_specs = [row(d)]
    out_shape = [jax.ShapeDtypeStruct((t, d), F32)]
    final_gain = final_norm if final_norm is not None else g
    if project is not None:
        assert mixed is None and final_norm is None
        g2, w_in, splits = project
        in_specs += [vec, resident(w_in.shape)]
        args += (g2, w_in)
        out_specs += [row(width) for width, _, _ in splits]
        out_shape += [jax.ShapeDtypeStruct((t, width), dt) for width, dt, _ in splits]
        body = functools.partial(_ffn_inproj_kernel, bounds=bounds, scales=tuple(s for _, _, s in splits))
        name = "ffn_inproj"
    elif mixed is not None:
        a, w_out = mixed
        in_specs = [row(a.shape[1]), resident(w_out.shape)] + in_specs + [vec]
        args = (a, w_out) + args + (final_gain,)
        body = functools.partial(_proj_ffn_kernel, final_norm=final_norm is not None, bounds=bounds)
        name = "proj_ffn"
    else:
        in_specs += [vec]
        args += (final_gain,)
        body = functools.partial(_ffn_kernel, final_norm=final_norm is not None, bounds=bounds)
        name = "ffn"
    out = pl.pallas_call(
        body,
        grid=(t // tm,),
        in_specs=in_specs,
        out_specs=out_specs,
        out_shape=out_shape,
        compiler_params=pltpu.CompilerParams(dimension_semantics=("arbitrary",),
                                             vmem_limit_bytes=BIG_VMEM_LIMIT),
        name=name,
    )(*args)
    return out[0] if project is None else out


def _inproj_even_kernel(x_ref, g_ref, w_ref, ucm_ref, q_ref, k_ref, v_ref, *, scale):
    tm = x_ref.shape[0]
    half = q_ref.shape[1]
    n_chunks = tm // CHUNK
    hn = _rms(x_ref[...], g_ref[...]).astype(BF16)
    u = _dot(hn, w_ref[:, :half])
    ut = [u[ch * CHUNK:(ch + 1) * CHUNK, :].T for ch in range(n_chunks)]
    q_ref[...] = (_dot(hn, w_ref[:, half:2 * half]) * scale).astype(q_ref.dtype)
    k_ref[...] = _dot(hn, w_ref[:, 2 * half:3 * half]).astype(k_ref.dtype)
    v_ref[...] = _dot(hn, w_ref[:, 3 * half:]).astype(v_ref.dtype)
    for jb in range(half // 8):
        rows = jnp.swapaxes(jnp.stack([t[jb * 8:(jb + 1) * 8, :] for t in ut], axis=0), 0, 1)
        for r in range(8):
            grp, chan = divmod(jb * 8 + r, S5_GROUP)
            ucm_ref[grp, :, chan * CHUNK:(chan + 1) * CHUNK] = rows[r]


def _inproj_even(x, g, w, scale):
    t, d = x.shape
    half = w.shape[1] // 4
    groups = half // S5_GROUP
    tm = EVEN_TOKEN_TILE
    n_chunks = tm // CHUNK
    tok = lambda width: pl.BlockSpec((tm, width), lambda i: (i, 0))
    return pl.pallas_call(
        functools.partial(_inproj_even_kernel, scale=scale),
        grid=(t // tm,),
        in_specs=[tok(d), pl.BlockSpec((1, d), lambda i: (0, 0)), pl.BlockSpec(w.shape, lambda i: (0, 0))],
        out_specs=[pl.BlockSpec((groups, n_chunks, S5_GROUP * CHUNK), lambda i: (0, i, 0)),
                   tok(half), tok(half), tok(half)],
        out_shape=[jax.ShapeDtypeStruct((groups, t // CHUNK, S5_GROUP * CHUNK), F32)] +
                  [jax.ShapeDtypeStruct((t, half), BF16)] * 3,
        compiler_params=_params(1),
        name="inproj_even",
    )(x, g, w)


def _cmul(ar, ai, br, bi):
    return ar * br - ai * bi, ar * bi + ai * br


def _discretise(lr, li, dt):
    mag = jnp.exp(lr * dt)
    ar = mag * jnp.cos(li * dt)
    ai = mag * jnp.sin(li * dt)
    den = lr * lr + li * li
    nr = ar - 1.0
    return ar, ai, (nr * lr + ai * li) / den, (ai * lr - nr * li) / den


def _powers(exponent, ar, ai, shape):
    pr = jnp.ones(shape, F32)
    pi = jnp.zeros(shape, F32)
    sr, si = ar, ai
    for j in range((2 * CHUNK).bit_length() - 1):
        bit = ((exponent >> j) & 1) == 1
        mr, mi = _cmul(pr, pi, sr, si)
        pr = jnp.where(bit, mr, pr)
        pi = jnp.where(bit, mi, pi)
        sr, si = _cmul(sr, si, sr, si)
    keep = exponent >= 0
    return jnp.where(keep, pr, 0.0), jnp.where(keep, pi, 0.0)


def _s5_table_kernel(lre_ref, lim_ref, ldt_ref, lre_col_ref, lim_col_ref, btr_ref, bti_ref, cr_ref, ci_ref,
                     ctr_ref, cti_ref, m_ref, zb_ref, oc_ref, aq_ref, k2_scr):
    q = CHUNK
    p = S5_STATE
    c = S5_GROUP
    lane2 = lax.broadcasted_iota(jnp.int32, (1, 2 * q), 1)
    lane1 = lax.broadcasted_iota(jnp.int32, (1, q), 1)
    row1 = lax.broadcasted_iota(jnp.int32, (q, 1), 0)

    kappa = None
    for d in range(2):
        dt = jnp.exp(ldt_ref[d, 0])
        ar, ai, zr, zi = _discretise(lre_ref[d, 0], lim_ref[d, 0], dt)
        acr, aci, _, _ = _discretise(lre_col_ref[d, 0], lim_col_ref[d, 0], dt)
        btr = btr_ref[d, 0]
        bti = bti_ref[d, 0]
        bbr = zr * btr - zi * bti
        bbi = zr * bti + zi * btr
        cre = cr_ref[d, 0]
        cim = ci_ref[d, 0]

        rep = lambda a: jnp.concatenate([jnp.broadcast_to(a[i:i + 1], (c, p)) for i in range(c)], axis=0)
        til = lambda a: jnp.concatenate([a] * c, axis=0)
        wr, wi = _cmul(rep(bbr), rep(bbi), til(cre), til(cim))
        lag = (lane2 - q) if d == 0 else (q - lane2)
        pr, pi = _powers(lag, acr, aci, (p, 2 * q))
        part = jnp.dot(wr, pr, precision=_HI, preferred_element_type=F32) - \
            jnp.dot(wi, pi, precision=_HI, preferred_element_type=F32)
        kappa = part if kappa is None else kappa + part

        zpr, zpi = _powers((q - 1 - row1) if d == 0 else row1, ar, ai, (q, p))
        for i in range(c):
            zbr, zbi = _cmul(bbr[i:i + 1], bbi[i:i + 1], zpr, zpi)
            zb_ref[0, i * q:(i + 1) * q, d * p:(d + 1) * p] = zbr.astype(BF16)
            zb_ref[0, i * q:(i + 1) * q, 2 * p + d * p:2 * p + (d + 1) * p] = zbi.astype(BF16)

        opr, opi = _powers((lane1 + 1) if d == 0 else (q - lane1), acr, aci, (p, q))
        ctr = ctr_ref[d, 0]
        cti = cti_ref[d, 0]
        for i in range(c):
            e_re, e_im = _cmul(ctr[:, i:i + 1], cti[:, i:i + 1], opr, opi)
            oc_ref[0, d * p:(d + 1) * p, i * q:(i + 1) * q] = e_re.astype(BF16)
            oc_ref[0, 2 * p + d * p:2 * p + (d + 1) * p, i * q:(i + 1) * q] = (-e_im).astype(BF16)

        aqr, aqi = _powers(jnp.full((1, 1), q, jnp.int32), ar, ai, (1, p))
        aq_ref[0, :, d * p:(d + 1) * p] = aqr
        aq_ref[0, :, 2 * p + d * p:2 * p + (d + 1) * p] = aqi

    k2_scr[...] = kappa

    def fill(i, carry):
        for j in range(c):
            row = k2_scr[pl.ds(i * c + j, 1), :]
            toe = pltpu.roll(jnp.broadcast_to(row, (q, 2 * q)), q, axis=1, stride=1, stride_axis=0)
            m_ref[0, pl.ds(pl.multiple_of(i * q, q), q), j * q:(j + 1) * q] = toe[:, :q].astype(BF16)
        return carry

    lax.fori_loop(0, c, fill, 0)


def _s5_tables(lam_re, lam_im, log_dt, b_re, b_im, c_re, c_im):
    g = lam_re.shape[1]
    p, c, q = S5_STATE, S5_GROUP, CHUNK
    swap = lambda a: jnp.swapaxes(a, -1, -2)
    spec4 = lambda r, w: pl.BlockSpec((2, 1, r, w), lambda i: (0, i, 0, 0))
    return pl.pallas_call(
        _s5_table_kernel,
        grid=(g,),
        in_specs=[spec4(1, p), spec4(1, p), spec4(1, 1), spec4(p, 1), spec4(p, 1),
                  spec4(c, p), spec4(c, p), spec4(c, p), spec4(c, p), spec4(p, c), spec4(p, c)],
        out_specs=[
            pl.BlockSpec((1, q * c, q * c), lambda i: (i, 0, 0)),
            pl.BlockSpec((1, q * c, 4 * p), lambda i: (i, 0, 0)),
            pl.BlockSpec((1, 4 * p, q * c), lambda i: (i, 0, 0)),
            pl.BlockSpec((1, 1, 4 * p), lambda i: (i, 0, 0)),
        ],
        out_shape=[
            jax.ShapeDtypeStruct((g, q * c, q * c), BF16),
            jax.ShapeDtypeStruct((g, q * c, 4 * p), BF16),
            jax.ShapeDtypeStruct((g, 4 * p, q * c), BF16),
            jax.ShapeDtypeStruct((g, 1, 4 * p), F32),
        ],
        scratch_shapes=[pltpu.VMEM((c * c, 2 * q), F32)],
        compiler_params=_params(1),
        name="s5_tables",
    )(lam_re.reshape(2, g, 1, p), lam_im.reshape(2, g, 1, p), log_dt.reshape(2, g, 1, 1),
      lam_re.reshape(2, g, p, 1), lam_im.reshape(2, g, p, 1),
      swap(b_re), swap(b_im), c_re, c_im, swap(c_re), swap(c_im))


def _s5_kernel(u_ref, d_ref, m_ref, zb_ref, oc_ref, aq_ref, y_ref,
               zr_scr, zi_scr, fr_scr, fi_scr, br_scr, bi_scr, *, nc, bsz):
    half = 2 * S5_STATE
    uf = u_ref[0]
    u = uf.astype(BF16)
    z = _dot(u, zb_ref[0])
    zr_scr[...] = z[:, :half]
    zi_scr[...] = z[:, half:]
    aq = aq_ref[0]
    ar, ai = aq[:, :half], aq[:, half:]
    zero = jnp.zeros((bsz, half), F32)

    def step(j, carry):
        fr, fi, br, bi = carry
        rows_f = pl.ds(j, bsz, stride=nc)
        rows_b = pl.ds(nc - 1 - j, bsz, stride=nc)
        fr_scr[rows_f, :] = fr
        fi_scr[rows_f, :] = fi
        br_scr[rows_b, :] = br
        bi_scr[rows_b, :] = bi
        nfr, nfi = _cmul(ar, ai, fr, fi)
        nbr, nbi = _cmul(ar, ai, br, bi)
        return (nfr + zr_scr[rows_f, :], nfi + zi_scr[rows_f, :],
                nbr + zr_scr[rows_b, :], nbi + zi_scr[rows_b, :])

    lax.fori_loop(0, nc, step, (zero, zero, zero, zero))
    is_fwd = lax.broadcasted_iota(jnp.int32, (1, half), 1) < S5_STATE
    s = jnp.concatenate([jnp.where(is_fwd, fr_scr[...], br_scr[...]),
                         jnp.where(is_fwd, fi_scr[...], bi_scr[...])], axis=1).astype(BF16)
    y_ref[0] = d_ref[0] * uf + _dot(u, m_ref[0]) + _dot(s, oc_ref[0])


def _s5_apply(u_cm, d_cm, m, zb, oc, aq, nc, bsz):
    g, rows, w = u_cm.shape
    st = zb.shape[2]
    return pl.pallas_call(
        functools.partial(_s5_kernel, nc=nc, bsz=bsz),
        grid=(g,),
        in_specs=[
            pl.BlockSpec((1, rows, w), lambda i: (i, 0, 0)),
            pl.BlockSpec((1, 1, w), lambda i: (i, 0, 0)),
            pl.BlockSpec((1, w, w), lambda i: (i, 0, 0)),
            pl.BlockSpec((1, w, st), lambda i: (i, 0, 0)),
            pl.BlockSpec((1, st, w), lambda i: (i, 0, 0)),
            pl.BlockSpec((1, 1, st), lambda i: (i, 0, 0)),
        ],
        out_specs=pl.BlockSpec((1, rows, w), lambda i: (i, 0, 0)),
        out_shape=jax.ShapeDtypeStruct((g, rows, w), F32),
        scratch_shapes=[pltpu.VMEM((rows, st // 2), F32)] * 6,
        compiler_params=pltpu.CompilerParams(dimension_semantics=("arbitrary",),
                                             vmem_limit_bytes=BIG_VMEM_LIMIT),
        name="s5_apply",
    )(u_cm, d_cm, m, zb, oc, aq)


def _na_bias_kernel(rpb_ref, o_ref):
    st = pl.program_id(0)
    h = pl.program_id(1)
    n_rel_rows = 2 * NA_ROWS - 1
    n_rel_cols = 2 * NA_COLS - 1
    w = lax.broadcasted_iota(jnp.int32, (GRID_W, LANES), 0)
    lane = lax.broadcasted_iota(jnp.int32, (GRID_W, LANES), 1)
    wk = lane & (GRID_W - 1)
    first = lane < GRID_W
    cs = jnp.clip(w - NA_COLS // 2, 0, GRID_W - NA_COLS)
    valid = (wk >= cs) & (wk < cs + NA_COLS)
    rel = wk - w + NA_COLS - 1
    for i in range(NA_ROWS * GRID_W // LANES):
        base0 = (h * n_rel_rows + st + 2 * i) * n_rel_cols
        base1 = base0 + n_rel_cols
        acc = jnp.full((GRID_W, LANES), NEG_INF, F32)
        for j in range(n_rel_cols):
            val = jnp.where(first, rpb_ref[base0 + j], rpb_ref[base1 + j]) * LOG2E
            acc = jnp.where(valid & (rel == j), val, acc)
        o_ref[0, 0, :, i * LANES:(i + 1) * LANES] = acc


def _na_bias(rpb):
    h = rpb.shape[0]
    pairs = h // 2
    return pl.pallas_call(
        _na_bias_kernel,
        grid=(NA_ROWS, h),
        in_specs=[pl.BlockSpec(memory_space=pltpu.SMEM)],
        out_specs=pl.BlockSpec((1, 1, GRID_W, NA_ROWS * GRID_W), lambda s, i: (s, i // 2, i % 2, 0)),
        out_shape=jax.ShapeDtypeStruct((NA_ROWS, pairs, 2 * GRID_W, NA_ROWS * GRID_W), F32),
        compiler_params=_params(2),
        name="na_bias",
    )(rpb.reshape(-1))


def _na_kernel(q_ref, k_ref, v_ref, b_ref, o_ref, s_scr, p_scr, *, rows):
    lane = lax.broadcasted_iota(jnp.int32, (GRID_W, LANES), 1)
    first = lane < NA_HEAD_DIM
    n_keys = NA_ROWS * GRID_W
    ones = jnp.ones((n_keys, LANES), BF16)

    def row_start(r):
        return jnp.clip(r - NA_ROWS // 2, 0, rows - NA_ROWS)

    def row_group(i, carry):
        base = i * NA_ROWS_PER_STEP
        for j in range(NA_ROWS_PER_STEP):
            r = base + j
            rs = row_start(r)
            q = q_ref[0, r]
            zero = jnp.zeros_like(q)
            qs = jnp.concatenate([jnp.where(first, q, zero), jnp.where(first, zero, q)], axis=0)
            kw = k_ref[0, pl.ds(rs, NA_ROWS)].reshape(n_keys, LANES)
            s_scr[j] = _dot_nt(qs, kw) + b_ref[rs - r + NA_ROWS - 1, 0]
        s = s_scr[...]
        p_scr[...] = jnp.exp2(s - jnp.max(s, axis=-1, keepdims=True)).astype(BF16)
        for j in range(NA_ROWS_PER_STEP):
            r = base + j
            vw = v_ref[0, pl.ds(row_start(r), NA_ROWS)].reshape(n_keys, LANES)
            on = _dot(p_scr[j], jnp.concatenate([vw, ones], axis=1))
            o = on[:, :LANES] / on[:, LANES:]
            o_ref[0, r] = jnp.where(first, o[:GRID_W], o[GRID_W:]).astype(o_ref.dtype)
        return carry

    lax.fori_loop(0, rows // NA_ROWS_PER_STEP, row_group, 0)


def _na(q, k, v, bias, bsz, rows):
    t, width = q.shape
    pairs = width // LANES
    shape4 = (bsz, rows, GRID_W, width)
    blk = pl.BlockSpec((1, rows, GRID_W, LANES), lambda b, i: (b, 0, 0, i))
    out = pl.pallas_call(
        functools.partial(_na_kernel, rows=rows),
        grid=(bsz, pairs),
        in_specs=[blk, blk, blk,
                  pl.BlockSpec((NA_ROWS, 1, 2 * GRID_W, NA_ROWS * GRID_W), lambda b, i: (0, i, 0, 0))],
        out_specs=blk,
        out_shape=jax.ShapeDtypeStruct(shape4, BF16),
        scratch_shapes=[pltpu.VMEM((NA_ROWS_PER_STEP, 2 * GRID_W, NA_ROWS * GRID_W), F32),
                        pltpu.VMEM((NA_ROWS_PER_STEP, 2 * GRID_W, NA_ROWS * GRID_W), BF16)],
        compiler_params=_params(2),
        name="na_attn",
    )(q.reshape(shape4), k.reshape(shape4), v.reshape(shape4), bias)
    return out.reshape(t, width)


def _out_even_kernel(ycm_ref, na_ref, x_ref, wglu_ref, bglu_ref, wo_ref, o_ref):
    groups, n_chunks, _ = ycm_ref.shape
    half = groups * S5_GROUP
    per_chunk = [[] for _ in range(n_chunks)]
    for jb in range(half // 8):
        tiles = []
        for r in range(8):
            grp, chan = divmod(jb * 8 + r, S5_GROUP)
            tiles.append(ycm_ref[grp, :, chan * CHUNK:(chan + 1) * CHUNK])
        rows = jnp.swapaxes(jnp.stack(tiles, axis=0), 0, 1)
        for ch in range(n_chunks):
            per_chunk[ch].append(rows[ch])
    y = jnp.concatenate([jnp.concatenate(blocks, axis=0).T for blocks in per_chunk], axis=0)
    g = jax.nn.gelu(y)
    a = g * jax.nn.sigmoid(_dot(g.astype(BF16), wglu_ref[...]) + bglu_ref[...])
    mix = _dot(a.astype(BF16), wo_ref[:half, :]) + _dot(na_ref[...], wo_ref[half:, :])
    o_ref[...] = x_ref[...] + mix


def _out_even(y_cm, na, x, w_glu, b_glu, w_out):
    t, d = x.shape
    groups = y_cm.shape[0]
    half = groups * S5_GROUP
    tm = EVEN_TOKEN_TILE
    n_chunks = tm // CHUNK
    row = lambda w: pl.BlockSpec((tm, w), lambda i: (i, 0))
    full = lambda r, w: pl.BlockSpec((r, w), lambda i: (0, 0))
    return pl.pallas_call(
        _out_even_kernel,
        grid=(t // tm,),
        in_specs=[pl.BlockSpec((groups, n_chunks, S5_GROUP * CHUNK), lambda i: (0, i, 0)),
                  row(half), row(d), full(half, half), full(1, half), full(d, d)],
        out_specs=row(d),
        out_shape=jax.ShapeDtypeStruct((t, d), F32),
        compiler_params=_params(1),
        name="out_even",
    )(y_cm, na, x, w_glu, b_glu, w_out)


def _t5_index_table():
    half = T5_BUCKETS // 2
    max_exact = half // 2
    rel = np.arange(3 * BLOCK)[None, :] - BLOCK - np.arange(BLOCK)[:, None]
    n = np.abs(rel)
    assert T5_MAX_DIST // max_exact == 16 and half - max_exact == 8 and max_exact == 8
    sq = np.maximum(n, 1).astype(np.int64) ** 2
    log2_sq = np.floor(np.log2(sq.astype(np.float64) + 0.5)).astype(np.int64)
    large = np.minimum(max_exact + log2_sq - 6, half - 1)
    bucket = np.where(rel > 0, half, 0) + np.where(n < max_exact, n, large)
    return np.where(n <= WINDOW, bucket, -1).astype(np.int32)


def _t5_bias_kernel(idx_ref, t5_ref, o_ref, *, heads):
    variant = pl.program_id(0)
    h = pl.program_id(1)
    idx = idx_ref[...]
    acc = jnp.full(idx.shape, NEG_INF, F32)
    for b in range(T5_BUCKETS):
        acc = jnp.where(idx == b, t5_ref[b * heads + h] * LOG2E, acc)
    col = lax.broadcasted_iota(jnp.int32, idx.shape, 1)
    absent = ((variant == 1) & (col < BLOCK)) | ((variant == 2) & (col >= 2 * BLOCK))
    o_ref[0, 0] = jnp.where(absent, NEG_INF, acc)


def _t5_bias(t5_table):
    heads = t5_table.shape[1]
    idx = jnp.asarray(_t5_index_table())
    per_pair = 2 * GQA_GROUP
    return pl.pallas_call(
        functools.partial(_t5_bias_kernel, heads=heads),
        grid=(3, heads),
        in_specs=[pl.BlockSpec((BLOCK, 3 * BLOCK), lambda e, h: (0, 0)),
                  pl.BlockSpec(memory_space=pltpu.SMEM)],
        out_specs=pl.BlockSpec(
            (1, 1, BLOCK, 3 * BLOCK),
            lambda e, h: (e, (h // per_pair) * GQA_GROUP + h % GQA_GROUP, (h % per_pair) // GQA_GROUP, 0)),
        out_shape=jax.ShapeDtypeStruct((3, heads // 2, 2 * BLOCK, 3 * BLOCK), F32),
        compiler_params=_params(2),
        name="t5_bias",
    )(idx, t5_table.reshape(-1))


def _gqa_head_order(heads):
    order = []
    for tt in range(heads // 2):
        pair, t = divmod(tt, GQA_GROUP)
        order += [2 * GQA_GROUP * pair + t, 2 * GQA_GROUP * pair + GQA_GROUP + t]
    return order


def _gqa_kernel(q_ref, kp_ref, kc_ref, kn_ref, vp_ref, vc_ref, vn_ref, bias_a_ref, bias_b_ref, sink_ref, o_ref,
                s_scr, p_scr, *, heads):
    k_all = jnp.concatenate([kp_ref[...], kc_ref[...], kn_ref[...]], axis=0)
    v_all = jnp.concatenate([vp_ref[...], vc_ref[...], vn_ref[...]], axis=0)
    lane = lax.broadcasted_iota(jnp.int32, (BLOCK, LANES), 1)
    first = lane < GQA_HEAD_DIM
    upper = lax.broadcasted_iota(jnp.int32, (2 * BLOCK, 1), 0) < BLOCK
    ones = jnp.ones((3 * BLOCK, LANES), BF16)
    order = _gqa_head_order(heads)
    tiles = heads // 2
    for blk, bias_ref in enumerate((bias_a_ref, bias_b_ref)):
        rows = slice(blk * BLOCK, (blk + 1) * BLOCK)
        k = k_all[blk * BLOCK:(blk + 3) * BLOCK]
        v = v_all[blk * BLOCK:(blk + 3) * BLOCK]
        for tt in range(tiles):
            k2 = k[:, tt // GQA_GROUP * LANES:(tt // GQA_GROUP + 1) * LANES]
            tile = q_ref[rows, tt * LANES:(tt + 1) * LANES]
            zero = jnp.zeros_like(tile)
            qs = jnp.concatenate([jnp.where(first, tile, zero), jnp.where(first, zero, tile)], axis=0)
            s_scr[blk, tt] = _dot_nt(qs, k2) + bias_ref[0, tt]
        s = s_scr[blk]
        m = jnp.max(s, axis=-1, keepdims=True)
        p_scr[blk] = jnp.exp2(s - m).astype(BF16)
        for tt in range(tiles):
            pair = tt // GQA_GROUP
            v2 = jnp.concatenate([v[:, pair * LANES:(pair + 1) * LANES], ones], axis=1)
            on = _dot(p_scr[blk, tt], v2)
            sk = jnp.where(upper, sink_ref[order[2 * tt]], sink_ref[order[2 * tt + 1]]) * LOG2E
            o = on[:, :LANES] / (on[:, LANES:] + jnp.exp2(sk - m[tt]))
            o_ref[rows, tt * LANES:(tt + 1) * LANES] = \
                jnp.where(first, o[:BLOCK], o[BLOCK:]).astype(o_ref.dtype)


def _gqa(q, k, v, bias, sink, bsz, nb):
    t, qw = q.shape
    kw = k.shape[1]
    heads = qw // GQA_HEAD_DIM
    assert nb % 2 == 0, "query blocks are processed in pairs"
    steps = nb // 2
    cur = lambda w: pl.BlockSpec((2 * BLOCK, w), lambda b, j: (b * steps + j, 0))
    prev = lambda w: pl.BlockSpec((BLOCK, w), lambda b, j: (b * nb + jnp.maximum(2 * j - 1, 0), 0))
    nxt = lambda w: pl.BlockSpec((BLOCK, w), lambda b, j: (b * nb + jnp.minimum(2 * j + 2, nb - 1), 0))
    tile_bias = lambda variant: pl.BlockSpec((1, heads // 2, 2 * BLOCK, 3 * BLOCK),
                                             lambda b, j: (variant(j), 0, 0, 0))
    return pl.pallas_call(
        functools.partial(_gqa_kernel, heads=heads),
        grid=(bsz, steps),
        in_specs=[cur(qw), prev(kw), cur(kw), nxt(kw), prev(kw), cur(kw), nxt(kw),
                  tile_bias(lambda j: jnp.where(j == 0, 1, 0)),
                  tile_bias(lambda j: jnp.where(j == steps - 1, 2, 0)),
                  pl.BlockSpec(memory_space=pltpu.SMEM)],
        out_specs=cur(qw),
        out_shape=jax.ShapeDtypeStruct((t, qw), BF16),
        scratch_shapes=[pltpu.VMEM((2, heads // 2, 2 * BLOCK, 3 * BLOCK), F32),
                        pltpu.VMEM((2, heads // 2, 2 * BLOCK, 3 * BLOCK), BF16)],
        compiler_params=_params(2),
        name="gqa_attn",
    )(q, k, k, k, v, v, v, bias, bias, sink)


def _prepare(p):
    depth = p["norm_mix"].shape[0]
    row = lambda a: a.reshape(1, -1).astype(F32)
    prep = {"depth": depth, "layers": []}
    heads = p["gqa_sink"].shape[-1] if depth > 1 else 0
    if depth > 1:
        order = np.asarray(_gqa_head_order(heads))
        slot_cols = (order[:, None] * GQA_HEAD_DIM + np.arange(GQA_HEAD_DIM)[None, :]).reshape(-1)
        prep["t5_bias"] = _t5_bias(p["t5_table"].astype(F32))
    for layer in range(depth):
        i = layer // 2
        lp = {
            "ffn": [(row(p["norm_ffn"][layer, j]), p["w_ffn_gate"][layer, j].astype(BF16),
                     p["w_ffn_up"][layer, j].astype(BF16), p["w_ffn_down"][layer, j].astype(BF16))
                    for j in range(2)],
            "norm_mix": row(p["norm_mix"][layer]),
        }
        if layer % 2 == 0:
            lp["w_in"] = p["w_in_even"][i].astype(BF16)
            lp["tables"] = _s5_tables(p["s5_lam_re"][i].astype(F32), p["s5_lam_im"][i].astype(F32),
                                      p["s5_log_dt"][i].astype(F32), p["s5_b_re"][i].astype(F32),
                                      p["s5_b_im"][i].astype(F32), p["s5_c_re"][i].astype(F32),
                                      p["s5_c_im"][i].astype(F32))
            groups = p["s5_lam_re"].shape[2]
            lp["d_skip"] = jnp.repeat(p["s5_d"][i].astype(F32).reshape(groups, 1, S5_GROUP), CHUNK, axis=2)
            lp["w_glu"] = p["s5_w_glu"][i].astype(BF16)
            lp["b_glu"] = row(p["s5_b_glu"][i])
            lp["na_bias"] = _na_bias(p["na_rpb"][i].astype(F32))
            lp["w_out"] = p["w_out_even"][i].astype(BF16)
        else:
            w_in = p["w_in_odd"][i]
            nq = heads * GQA_HEAD_DIM
            lp["w_in"] = jnp.concatenate([w_in[:, slot_cols], w_in[:, nq:]], axis=1).astype(BF16)
            lp["sink"] = p["gqa_sink"][i].astype(F32)
            lp["w_out"] = p["w_out_odd"][i][slot_cols, :].astype(BF16)
        prep["layers"].append(lp)
    prep["norm_final"] = row(p["norm_final"])
    return prep


def _trunk(x, prep):
    bsz, seq, d = x.shape
    t = bsz * seq
    x = x.reshape(t, d).astype(F32)
    depth = prep["depth"]
    for layer, lp in enumerate(prep["layers"]):
        if layer % 2 == 0:
            x = _ffn(x, *lp["ffn"][0])
            u_cm, q, k, v = _inproj_even(x, lp["norm_mix"], lp["w_in"], NA_HEAD_DIM ** -0.5 * LOG2E)
            y_cm = _s5_apply(u_cm, lp["d_skip"], *lp["tables"], seq // CHUNK, bsz)
            na = _na(q, k, v, lp["na_bias"], bsz, seq // GRID_W)
            x = _out_even(y_cm, na, x, lp["w_glu"], lp["b_glu"], lp["w_out"])
            mixed = None
        else:
            heads = lp["sink"].shape[0]
            nq = heads * GQA_HEAD_DIM
            nkv = nq // GQA_GROUP
            scale = GQA_HEAD_DIM ** -0.5 * LOG2E
            x, q, k, v = _ffn(x, *lp["ffn"][0], project=(
                lp["norm_mix"], lp["w_in"], [(nq, BF16, scale), (nkv, BF16, 1.0), (nkv, BF16, 1.0)]))
            mixed = (_gqa(q, k, v, prep["t5_bias"], lp["sink"], bsz, seq // BLOCK), lp["w_out"])
        x = _ffn(x, *lp["ffn"][1], mixed=mixed, final_norm=prep["norm_final"] if layer == depth - 1 else None)
    return x.reshape(bsz, seq, d)


def kernel(x_prompt, x_sample, norm_ffn, w_ffn_gate, w_ffn_up, w_ffn_down, norm_mix, w_in_even, s5_lam_re, s5_lam_im, s5_log_dt, s5_b_re, s5_b_im, s5_c_re, s5_c_im, s5_d, s5_w_glu, s5_b_glu, na_rpb, w_out_even, w_in_odd, gqa_sink, w_out_odd, t5_table, norm_final):
    prep = _prepare(dict(
        norm_ffn=norm_ffn, w_ffn_gate=w_ffn_gate, w_ffn_up=w_ffn_up, w_ffn_down=w_ffn_down,
        norm_mix=norm_mix, w_in_even=w_in_even, s5_lam_re=s5_lam_re, s5_lam_im=s5_lam_im,
        s5_log_dt=s5_log_dt, s5_b_re=s5_b_re, s5_b_im=s5_b_im, s5_c_re=s5_c_re, s5_c_im=s5_c_im,
        s5_d=s5_d, s5_w_glu=s5_w_glu, s5_b_glu=s5_b_glu, na_rpb=na_rpb, w_out_even=w_out_even,
        w_in_odd=w_in_odd, gqa_sink=gqa_sink, w_out_odd=w_out_odd, t5_table=t5_table,
        norm_final=norm_final))
    return (_trunk(x_prompt, prep), _trunk(x_sample, prep))
```

```python
import functools
import math

import jax
import jax.numpy as jnp
import numpy as np
from jax import lax
from jax.experimental import pallas as pl
from jax.experimental.pallas import tpu as pltpu

F32 = jnp.float32
BF16 = jnp.bfloat16

S5_GROUP = 16
S5_STATE = 64
NA_HEAD_DIM = 64
NA_ROWS = 8
NA_COLS = 16
GRID_W = 64
GQA_HEAD_DIM = 64
GQA_GROUP = 4
WINDOW = 128
BLOCK = 128
T5_BUCKETS = 32
T5_MAX_DIST = 128
RMS_EPS = 1e-6
NEG_INF = -1e30
LOG2E = math.log2(math.e)

LANES = 128
CHUNK = LANES
EVEN_TOKEN_TILE = 8 * CHUNK
FFN_TOKEN_TILE = 1024
FFN_CHUNK = 512
FFN_ROW_SPLITS = 2
NA_ROWS_PER_STEP = 16
GQA_BLOCKS_PER_STEP = 4
VMEM_LIMIT = 48 * 1024 * 1024
BIG_VMEM_LIMIT = 60 * 1024 * 1024

_NT = (((1,), (1,)), ((), ()))
_HI = lax.Precision.HIGHEST


def _params(n_axes):
    return pltpu.CompilerParams(dimension_semantics=("arbitrary",) * n_axes,
                                vmem_limit_bytes=VMEM_LIMIT)


def _rms(x, g):
    return x * lax.rsqrt(jnp.mean(x * x, axis=-1, keepdims=True) + RMS_EPS) * g


def _dot(a, b):
    return jnp.dot(a, b, preferred_element_type=F32)


def _dot_nt(a, b, precision=None):
    return lax.dot_general(a, b, _NT, precision=precision, preferred_element_type=F32)


def _half_ffn(x, g_ref, wg_ref, wu_ref, wd_ref, bounds):
    outs = []
    for xs in jnp.split(x, FFN_ROW_SPLITS, axis=0):
        hn = _rms(xs, g_ref[...]).astype(BF16)
        acc = None
        for lo, hi in bounds:
            gate = _dot(hn, wg_ref[:, lo:hi])
            up = _dot(hn, wu_ref[:, lo:hi])
            h = (gate * jax.nn.sigmoid(gate) * up).astype(BF16)
            part = _dot(h, wd_ref[lo:hi, :])
            acc = part if acc is None else acc + part
        outs.append(xs + 0.5 * acc)
    return jnp.concatenate(outs, axis=0)


def _project(x, g_ref, w_ref, o_refs, scales):
    hn = _rms(x, g_ref[...]).astype(BF16)
    off = 0
    for o_ref, scale in zip(o_refs, scales):
        width = o_ref.shape[1]
        part = _dot(hn, w_ref[:, off:off + width])
        if scale != 1.0:
            part = part * scale
        o_ref[...] = part.astype(o_ref.dtype)
        off += width


def _ffn_kernel(x_ref, g_ref, wg_ref, wu_ref, wd_ref, gf_ref, o_ref, *, final_norm, bounds):
    y = _half_ffn(x_ref[...], g_ref, wg_ref, wu_ref, wd_ref, bounds)
    o_ref[...] = _rms(y, gf_ref[...]) if final_norm else y


def _ffn_inproj_kernel(x_ref, g_ref, wg_ref, wu_ref, wd_ref, g2_ref, win_ref, o_ref, *p_refs, bounds, scales):
    y = _half_ffn(x_ref[...], g_ref, wg_ref, wu_ref, wd_ref, bounds)
    o_ref[...] = y
    _project(y, g2_ref, win_ref, p_refs, scales)


def _proj_ffn_kernel(a_ref, wo_ref, x_ref, g_ref, wg_ref, wu_ref, wd_ref, gf_ref, o_ref, *, final_norm, bounds):
    x = x_ref[...] + _dot(a_ref[...], wo_ref[...])
    y = _half_ffn(x, g_ref, wg_ref, wu_ref, wd_ref, bounds)
    o_ref[...] = _rms(y, gf_ref[...]) if final_norm else y


def _ffn(x, g, wg, wu, wd, *, mixed=None, project=None, final_norm=None):
    t, d = x.shape
    f = wg.shape[1]
    tm = FFN_TOKEN_TILE
    bounds = tuple((lo, min(lo + FFN_CHUNK, f)) for lo in range(0, f, FFN_CHUNK))
    resident = lambda shape: pl.BlockSpec(shape, lambda i: (0, 0), pipeline_mode=pl.Buffered(1))
    row = lambda width: pl.BlockSpec((tm, width), lambda i: (i, 0))
    vec = pl.BlockSpec((1, d), lambda i: (0, 0))
    in_specs = [row(d), vec, resident((d, f)), resident((d, f)), resident((f, d))]
    args = (x, g, wg, wu, wd)
    out_specs = [row(d)]
    out_shape = [jax.ShapeDtypeStruct((t, d), F32)]
    final_gain = final_norm if final_norm is not None else g
    if project is not None:
        assert mixed is None and final_norm is None
        g2, w_in, splits = project
        in_specs += [vec, resident(w_in.shape)]
        args += (g2, w_in)
        out_specs += [row(width) for width, _, _ in splits]
        out_shape += [jax.ShapeDtypeStruct((t, width), dt) for width, dt, _ in splits]
        body = functools.partial(_ffn_inproj_kernel, bounds=bounds, scales=tuple(s for _, _, s in splits))
        name = "ffn_inproj"
    elif mixed is not None:
        a, w_out = mixed
        in_specs = [row(a.shape[1]), resident(w_out.shape)] + in_specs + [vec]
        args = (a, w_out) + args + (final_gain,)
        body = functools.partial(_proj_ffn_kernel, final_norm=final_norm is not None, bounds=bounds)
        name = "proj_ffn"
    else:
        in_specs += [vec]
        args += (final_gain,)
        body = functools.partial(_ffn_kernel, final_norm=final_norm is not None, bounds=bounds)
        name = "ffn"
    out = pl.pallas_call(
        body,
        grid=(t // tm,),
        in_specs=in_specs,
        out_specs=out_specs,
        out_shape=out_shape,
        compiler_params=pltpu.CompilerParams(dimension_semantics=("arbitrary",),
                                             vmem_limit_bytes=BIG_VMEM_LIMIT),
        name=name,
    )(*args)
    return out[0] if project is None else out


def _inproj_even_kernel(x_ref, g_ref, w_ref, ucm_ref, q_ref, k_ref, v_ref, *, scale):
    tm = x_ref.shape[0]
    half = q_ref.shape[1]
    n_chunks = tm // CHUNK
    hn = _rms(x_ref[...], g_ref[...]).astype(BF16)
    u = _dot(hn, w_ref[:, :half])
    ut = [u[ch * CHUNK:(ch + 1) * CHUNK, :].T for ch in range(n_chunks)]
    q_ref[...] = (_dot(hn, w_ref[:, half:2 * half]) * scale).astype(q_ref.dtype)
    k_ref[...] = _dot(hn, w_ref[:, 2 * half:3 * half]).astype(k_ref.dtype)
    v_ref[...] = _dot(hn, w_ref[:, 3 * half:]).astype(v_ref.dtype)
    for jb in range(half // 8):
        rows = jnp.swapaxes(jnp.stack([t[jb * 8:(jb + 1) * 8, :] for t in ut], axis=0), 0, 1)
        for r in range(8):
            grp, chan = divmod(jb * 8 + r, S5_GROUP)
            ucm_ref[grp, :, chan * CHUNK:(chan + 1) * CHUNK] = rows[r]


def _inproj_even(x, g, w, scale):
    t, d = x.shape
    half = w.shape[1] // 4
    groups = half // S5_GROUP
    tm = EVEN_TOKEN_TILE
    n_chunks = tm // CHUNK
    tok = lambda width: pl.BlockSpec((tm, width), lambda i: (i, 0))
    return pl.pallas_call(
        functools.partial(_inproj_even_kernel, scale=scale),
        grid=(t // tm,),
        in_specs=[tok(d), pl.BlockSpec((1, d), lambda i: (0, 0)), pl.BlockSpec(w.shape, lambda i: (0, 0))],
        out_specs=[pl.BlockSpec((groups, n_chunks, S5_GROUP * CHUNK), lambda i: (0, i, 0)),
                   tok(half), tok(half), tok(half)],
        out_shape=[jax.ShapeDtypeStruct((groups, t // CHUNK, S5_GROUP * CHUNK), F32)] +
                  [jax.ShapeDtypeStruct((t, half), BF16)] * 3,
        compiler_params=_params(1),
        name="inproj_even",
    )(x, g, w)


def _cmul(ar, ai, br, bi):
    return ar * br - ai * bi, ar * bi + ai * br


def _discretise(lr, li, dt):
    mag = jnp.exp(lr * dt)
    ar = mag * jnp.cos(li * dt)
    ai = mag * jnp.sin(li * dt)
    den = lr * lr + li * li
    nr = ar - 1.0
    return ar, ai, (nr * lr + ai * li) / den, (ai * lr - nr * li) / den


def _powers(exponent, ar, ai, shape):
    pr = jnp.ones(shape, F32)
    pi = jnp.zeros(shape, F32)
    sr, si = ar, ai
    for j in range((2 * CHUNK).bit_length() - 1):
        bit = ((exponent >> j) & 1) == 1
        mr, mi = _cmul(pr, pi, sr, si)
        pr = jnp.where(bit, mr, pr)
        pi = jnp.where(bit, mi, pi)
        sr, si = _cmul(sr, si, sr, si)
    keep = exponent >= 0
    return jnp.where(keep, pr, 0.0), jnp.where(keep, pi, 0.0)


def _s5_table_kernel(lre_ref, lim_ref, ldt_ref, lre_col_ref, lim_col_ref, btr_ref, bti_ref, cr_ref, ci_ref,
                     ctr_ref, cti_ref, m_ref, zb_ref, oc_ref, aq_ref, k2_scr):
    q = CHUNK
    p = S5_STATE
    c = S5_GROUP
    lane2 = lax.broadcasted_iota(jnp.int32, (1, 2 * q), 1)
    lane1 = lax.broadcasted_iota(jnp.int32, (1, q), 1)
    row1 = lax.broadcasted_iota(jnp.int32, (q, 1), 0)

    kappa = None
    for d in range(2):
        dt = jnp.exp(ldt_ref[d, 0])
        ar, ai, zr, zi = _discretise(lre_ref[d, 0], lim_ref[d, 0], dt)
        acr, aci, _, _ = _discretise(lre_col_ref[d, 0], lim_col_ref[d, 0], dt)
        btr = btr_ref[d, 0]
        bti = bti_ref[d, 0]
        bbr = zr * btr - zi * bti
        bbi = zr * bti + zi * btr
        cre = cr_ref[d, 0]
        cim = ci_ref[d, 0]

        rep = lambda a: jnp.concatenate([jnp.broadcast_to(a[i:i + 1], (c, p)) for i in range(c)], axis=0)
        til = lambda a: jnp.concatenate([a] * c, axis=0)
        wr, wi = _cmul(rep(bbr), rep(bbi), til(cre), til(cim))
        lag = (lane2 - q) if d == 0 else (q - lane2)
        pr, pi = _powers(lag, acr, aci, (p, 2 * q))
        part = jnp.dot(wr, pr, precision=_HI, preferred_element_type=F32) - \
            jnp.dot(wi, pi, precision=_HI, preferred_element_type=F32)
        kappa = part if kappa is None else kappa + part

        zpr, zpi = _powers((q - 1 - row1) if d == 0 else row1, ar, ai, (q, p))
        for i in range(c):
            zbr, zbi = _cmul(bbr[i:i + 1], bbi[i:i + 1], zpr, zpi)
            zb_ref[0, i * q:(i + 1) * q, d * p:(d + 1) * p] = zbr.astype(BF16)
            zb_ref[0, i * q:(i + 1) * q, 2 * p + d * p:2 * p + (d + 1) * p] = zbi.astype(BF16)

        opr, opi = _powers((lane1 + 1) if d == 0 else (q - lane1), acr, aci, (p, q))
        ctr = ctr_ref[d, 0]
        cti = cti_ref[d, 0]
        for i in range(c):
            e_re, e_im = _cmul(ctr[:, i:i + 1], cti[:, i:i + 1], opr, opi)
            oc_ref[0, d * p:(d + 1) * p, i * q:(i + 1) * q] = e_re.astype(BF16)
            oc_ref[0, 2 * p + d * p:2 * p + (d + 1) * p, i * q:(i + 1) * q] = (-e_im).astype(BF16)

        aqr, aqi = _powers(jnp.full((1, 1), q, jnp.int32), ar, ai, (1, p))
        aq_ref[0, :, d * p:(d + 1) * p] = aqr
        aq_ref[0, :, 2 * p + d * p:2 * p + (d + 1) * p] = aqi

    k2_scr[...] = kappa

    def fill(i, carry):
        for j in range(c):
            row = k2_scr[pl.ds(i * c + j, 1), :]
            toe = pltpu.roll(jnp.broadcast_to(row, (q, 2 * q)), q, axis=1, stride=1, stride_axis=0)
            m_ref[0, pl.ds(pl.multiple_of(i * q, q), q), j * q:(j + 1) * q] = toe[:, :q].astype(BF16)
        return carry

    lax.fori_loop(0, c, fill, 0)


def _s5_tables(lam_re, lam_im, log_dt, b_re, b_im, c_re, c_im):
    g = lam_re.shape[1]
    p, c, q = S5_STATE, S5_GROUP, CHUNK
    swap = lambda a: jnp.swapaxes(a, -1, -2)
    spec4 = lambda r, w: pl.BlockSpec((2, 1, r, w), lambda i: (0, i, 0, 0))
    return pl.pallas_call(
        _s5_table_kernel,
        grid=(g,),
        in_specs=[spec4(1, p), spec4(1, p), spec4(1, 1), spec4(p, 1), spec4(p, 1),
                  spec4(c, p), spec4(c, p), spec4(c, p), spec4(c, p), spec4(p, c), spec4(p, c)],
        out_specs=[
            pl.BlockSpec((1, q * c, q * c), lambda i: (i, 0, 0)),
            pl.BlockSpec((1, q * c, 4 * p), lambda i: (i, 0, 0)),
            pl.BlockSpec((1, 4 * p, q * c), lambda i: (i, 0, 0)),
            pl.BlockSpec((1, 1, 4 * p), lambda i: (i, 0, 0)),
        ],
        out_shape=[
            jax.ShapeDtypeStruct((g, q * c, q * c), BF16),
            jax.ShapeDtypeStruct((g, q * c, 4 * p), BF16),
            jax.ShapeDtypeStruct((g, 4 * p, q * c), BF16),
            jax.ShapeDtypeStruct((g, 1, 4 * p), F32),
        ],
        scratch_shapes=[pltpu.VMEM((c * c, 2 * q), F32)],
        compiler_params=_params(1),
        name="s5_tables",
    )(lam_re.reshape(2, g, 1, p), lam_im.reshape(2, g, 1, p), log_dt.reshape(2, g, 1, 1),
      lam_re.reshape(2, g, p, 1), lam_im.reshape(2, g, p, 1),
      swap(b_re), swap(b_im), c_re, c_im, swap(c_re), swap(c_im))


def _s5_kernel(u_ref, d_ref, m_ref, zb_ref, oc_ref, aq_ref, y_ref, z_scr, sf_scr, sb_scr, *, nc, bsz):
    half = 2 * S5_STATE
    uf = u_ref[0]
    u = uf.astype(BF16)
    z_scr[...] = jnp.swapaxes(_dot(u, zb_ref[0]).reshape(bsz, nc, 2 * half), 0, 1)
    aq = aq_ref[0]
    ar, ai = aq[:, :half], aq[:, half:]
    zero = jnp.zeros((bsz, half), F32)

    def step(j, carry):
        fr, fi, br, bi = carry
        sf_scr[j, :, :half] = fr
        sf_scr[j, :, half:] = fi
        sb_scr[nc - 1 - j, :, :half] = br
        sb_scr[nc - 1 - j, :, half:] = bi
        zf = z_scr[j]
        zb = z_scr[nc - 1 - j]
        nfr, nfi = _cmul(ar, ai, fr, fi)
        nbr, nbi = _cmul(ar, ai, br, bi)
        return nfr + zf[:, :half], nfi + zf[:, half:], nbr + zb[:, :half], nbi + zb[:, half:]

    lax.fori_loop(0, nc, step, (zero, zero, zero, zero))
    lane = lax.broadcasted_iota(jnp.int32, (1, 1, 2 * half), 2)
    s = jnp.where((lane & S5_STATE) == 0, sf_scr[...], sb_scr[...])
    s = jnp.swapaxes(s, 0, 1).reshape(bsz * nc, 2 * half).astype(BF16)
    y_ref[0] = d_ref[0] * uf + _dot(u, m_ref[0]) + _dot(s, oc_ref[0])


def _s5_apply(u_cm, d_cm, m, zb, oc, aq, nc, bsz):
    g, rows, w = u_cm.shape
    st = zb.shape[2]
    return pl.pallas_call(
        functools.partial(_s5_kernel, nc=nc, bsz=bsz),
        grid=(g,),
        in_specs=[
            pl.BlockSpec((1, rows, w), lambda i: (i, 0, 0)),
            pl.BlockSpec((1, 1, w), lambda i: (i, 0, 0)),
            pl.BlockSpec((1, w, w), lambda i: (i, 0, 0)),
            pl.BlockSpec((1, w, st), lambda i: (i, 0, 0)),
            pl.BlockSpec((1, st, w), lambda i: (i, 0, 0)),
            pl.BlockSpec((1, 1, st), lambda i: (i, 0, 0)),
        ],
        out_specs=pl.BlockSpec((1, rows, w), lambda i: (i, 0, 0)),
        out_shape=jax.ShapeDtypeStruct((g, rows, w), F32),
        scratch_shapes=[pltpu.VMEM((nc, bsz, st), F32)] * 3,
        compiler_params=pltpu.CompilerParams(dimension_semantics=("arbitrary",),
                                             vmem_limit_bytes=BIG_VMEM_LIMIT),
        name="s5_apply",
    )(u_cm, d_cm, m, zb, oc, aq)


def _na_bias_kernel(rpb_ref, o_ref):
    st = pl.program_id(0)
    h = pl.program_id(1)
    n_rel_rows = 2 * NA_ROWS - 1
    n_rel_cols = 2 * NA_COLS - 1
    w = lax.broadcasted_iota(jnp.int32, (GRID_W, LANES), 0)
    lane = lax.broadcasted_iota(jnp.int32, (GRID_W, LANES), 1)
    wk = lane & (GRID_W - 1)
    first = lane < GRID_W
    cs = jnp.clip(w - NA_COLS // 2, 0, GRID_W - NA_COLS)
    valid = (wk >= cs) & (wk < cs + NA_COLS)
    rel = wk - w + NA_COLS - 1
    for i in range(NA_ROWS * GRID_W // LANES):
        base0 = (h * n_rel_rows + st + 2 * i) * n_rel_cols
        base1 = base0 + n_rel_cols
        acc = jnp.full((GRID_W, LANES), NEG_INF, F32)
        for j in range(n_rel_cols):
            val = jnp.where(first, rpb_ref[base0 + j], rpb_ref[base1 + j]) * LOG2E
            acc = jnp.where(valid & (rel == j), val, acc)
        o_ref[0, 0, :, i * LANES:(i + 1) * LANES] = acc


def _na_bias(rpb):
    h = rpb.shape[0]
    pairs = h // 2
    return pl.pallas_call(
        _na_bias_kernel,
        grid=(NA_ROWS, h),
        in_specs=[pl.BlockSpec(memory_space=pltpu.SMEM)],
        out_specs=pl.BlockSpec((1, 1, GRID_W, NA_ROWS * GRID_W), lambda s, i: (s, i // 2, i % 2, 0)),
        out_shape=jax.ShapeDtypeStruct((NA_ROWS, pairs, 2 * GRID_W, NA_ROWS * GRID_W), F32),
        compiler_params=_params(2),
        name="na_bias",
    )(rpb.reshape(-1))


def _na_kernel(q_ref, k_ref, v_ref, b_ref, o_ref, s_scr, p_scr, *, rows):
    lane = lax.broadcasted_iota(jnp.int32, (GRID_W, LANES), 1)
    first = lane < NA_HEAD_DIM
    n_keys = NA_ROWS * GRID_W
    ones = jnp.ones((n_keys, LANES), BF16)

    def row_start(r):
        return jnp.clip(r - NA_ROWS // 2, 0, rows - NA_ROWS)

    def row_group(i, carry):
        base = i * NA_ROWS_PER_STEP
        for j in range(NA_ROWS_PER_STEP):
            r = base + j
            rs = row_start(r)
            q = q_ref[0, r]
            zero = jnp.zeros_like(q)
            qs = jnp.concatenate([jnp.where(first, q, zero), jnp.where(first, zero, q)], axis=0)
            kw = k_ref[0, pl.ds(rs, NA_ROWS)].reshape(n_keys, LANES)
            s_scr[j] = _dot_nt(qs, kw) + b_ref[rs - r + NA_ROWS - 1, 0]
        s = s_scr[...]
        p_scr[...] = jnp.exp2(s - jnp.max(s, axis=-1, keepdims=True)).astype(BF16)
        for j in range(NA_ROWS_PER_STEP):
            r = base + j
            vw = v_ref[0, pl.ds(row_start(r), NA_ROWS)].reshape(n_keys, LANES)
            on = _dot(p_scr[j], jnp.concatenate([vw, ones], axis=1))
            o = on[:, :LANES] / on[:, LANES:]
            o_ref[0, r] = jnp.where(first, o[:GRID_W], o[GRID_W:]).astype(o_ref.dtype)
        return carry

    lax.fori_loop(0, rows // NA_ROWS_PER_STEP, row_group, 0)


def _na(q, k, v, bias, bsz, rows):
    t, width = q.shape
    pairs = width // LANES
    shape4 = (bsz, rows, GRID_W, width)
    blk = pl.BlockSpec((1, rows, GRID_W, LANES), lambda b, i: (b, 0, 0, i))
    out = pl.pallas_call(
        functools.partial(_na_kernel, rows=rows),
        grid=(bsz, pairs),
        in_specs=[blk, blk, blk,
                  pl.BlockSpec((NA_ROWS, 1, 2 * GRID_W, NA_ROWS * GRID_W), lambda b, i: (0, i, 0, 0))],
        out_specs=blk,
        out_shape=jax.ShapeDtypeStruct(shape4, BF16),
        scratch_shapes=[pltpu.VMEM((NA_ROWS_PER_STEP, 2 * GRID_W, NA_ROWS * GRID_W), F32),
                        pltpu.VMEM((NA_ROWS_PER_STEP, 2 * GRID_W, NA_ROWS * GRID_W), BF16)],
        compiler_params=_params(2),
        name="na_attn",
    )(q.reshape(shape4), k.reshape(shape4), v.reshape(shape4), bias)
    return out.reshape(t, width)


def _out_even_kernel(ycm_ref, na_ref, x_ref, wglu_ref, bglu_ref, wo_ref, o_ref):
    groups, n_chunks, _ = ycm_ref.shape
    half = groups * S5_GROUP
    per_chunk = [[] for _ in range(n_chunks)]
    for jb in range(half // 8):
        tiles = []
        for r in range(8):
            grp, chan = divmod(jb * 8 + r, S5_GROUP)
            tiles.append(ycm_ref[grp, :, chan * CHUNK:(chan + 1) * CHUNK])
        rows = jnp.swapaxes(jnp.stack(tiles, axis=0), 0, 1)
        for ch in range(n_chunks):
            per_chunk[ch].append(rows[ch])
    y = jnp.concatenate([jnp.concatenate(blocks, axis=0).T for blocks in per_chunk], axis=0)
    g = jax.nn.gelu(y)
    a = g * jax.nn.sigmoid(_dot(g.astype(BF16), wglu_ref[...]) + bglu_ref[...])
    mix = _dot(a.astype(BF16), wo_ref[:half, :]) + _dot(na_ref[...], wo_ref[half:, :])
    o_ref[...] = x_ref[...] + mix


def _out_even(y_cm, na, x, w_glu, b_glu, w_out):
    t, d = x.shape
    groups = y_cm.shape[0]
    half = groups * S5_GROUP
    tm = EVEN_TOKEN_TILE
    n_chunks = tm // CHUNK
    row = lambda w: pl.BlockSpec((tm, w), lambda i: (i, 0))
    full = lambda r, w: pl.BlockSpec((r, w), lambda i: (0, 0))
    return pl.pallas_call(
        _out_even_kernel,
        grid=(t // tm,),
        in_specs=[pl.BlockSpec((groups, n_chunks, S5_GROUP * CHUNK), lambda i: (0, i, 0)),
                  row(half), row(d), full(half, half), full(1, half), full(d, d)],
        out_specs=row(d),
        out_shape=jax.ShapeDtypeStruct((t, d), F32),
        compiler_params=_params(1),
        name="out_even",
    )(y_cm, na, x, w_glu, b_glu, w_out)


def _t5_index_table():
    half = T5_BUCKETS // 2
    max_exact = half // 2
    rel = np.arange(3 * BLOCK)[None, :] - BLOCK - np.arange(BLOCK)[:, None]
    n = np.abs(rel)
    assert T5_MAX_DIST // max_exact == 16 and half - max_exact == 8 and max_exact == 8
    sq = np.maximum(n, 1).astype(np.int64) ** 2
    log2_sq = np.floor(np.log2(sq.astype(np.float64) + 0.5)).astype(np.int64)
    large = np.minimum(max_exact + log2_sq - 6, half - 1)
    bucket = np.where(rel > 0, half, 0) + np.where(n < max_exact, n, large)
    return np.where(n <= WINDOW, bucket, -1).astype(np.int32)


def _t5_bias_kernel(idx_ref, t5_ref, o_ref, *, heads):
    variant = pl.program_id(0)
    h = pl.program_id(1)
    idx = idx_ref[...]
    acc = jnp.full(idx.shape, NEG_INF, F32)
    for b in range(T5_BUCKETS):
        acc = jnp.where(idx == b, t5_ref[b * heads + h] * LOG2E, acc)
    col = lax.broadcasted_iota(jnp.int32, idx.shape, 1)
    absent = ((variant == 1) & (col < BLOCK)) | ((variant == 2) & (col >= 2 * BLOCK))
    o_ref[0, 0] = jnp.where(absent, NEG_INF, acc)


def _t5_bias(t5_table):
    heads = t5_table.shape[1]
    idx = jnp.asarray(_t5_index_table())
    per_pair = 2 * GQA_GROUP
    return pl.pallas_call(
        functools.partial(_t5_bias_kernel, heads=heads),
        grid=(3, heads),
        in_specs=[pl.BlockSpec((BLOCK, 3 * BLOCK), lambda e, h: (0, 0)),
                  pl.BlockSpec(memory_space=pltpu.SMEM)],
        out_specs=pl.BlockSpec(
            (1, 1, BLOCK, 3 * BLOCK),
            lambda e, h: (e, (h // per_pair) * GQA_GROUP + h % GQA_GROUP, (h % per_pair) // GQA_GROUP, 0)),
        out_shape=jax.ShapeDtypeStruct((3, heads // 2, 2 * BLOCK, 3 * BLOCK), F32),
        compiler_params=_params(2),
        name="t5_bias",
    )(idx, t5_table.reshape(-1))


def _gqa_head_order(heads):
    order = []
    for tt in range(heads // 2):
        pair, t = divmod(tt, GQA_GROUP)
        order += [2 * GQA_GROUP * pair + t, 2 * GQA_GROUP * pair + GQA_GROUP + t]
    return order


def _gqa_kernel(q_ref, kp_ref, kc_ref, kn_ref, vp_ref, vc_ref, vn_ref, bias_first_ref, bias_mid_ref, bias_last_ref,
                sink_ref, o_ref, s_scr, p_scr, *, heads):
    k_all = jnp.concatenate([kp_ref[...], kc_ref[...], kn_ref[...]], axis=0)
    v_all = jnp.concatenate([vp_ref[...], vc_ref[...], vn_ref[...]], axis=0)
    lane = lax.broadcasted_iota(jnp.int32, (BLOCK, LANES), 1)
    first = lane < GQA_HEAD_DIM
    upper = lax.broadcasted_iota(jnp.int32, (2 * BLOCK, 1), 0) < BLOCK
    ones = jnp.ones((3 * BLOCK, LANES), BF16)
    order = _gqa_head_order(heads)
    tiles = heads // 2
    per = GQA_BLOCKS_PER_STEP
    for blk in range(per):
        bias_ref = bias_first_ref if blk == 0 else bias_last_ref if blk == per - 1 else bias_mid_ref
        rows = slice(blk * BLOCK, (blk + 1) * BLOCK)
        k = k_all[blk * BLOCK:(blk + 3) * BLOCK]
        v = v_all[blk * BLOCK:(blk + 3) * BLOCK]
        for tt in range(tiles):
            k2 = k[:, tt // GQA_GROUP * LANES:(tt // GQA_GROUP + 1) * LANES]
            tile = q_ref[rows, tt * LANES:(tt + 1) * LANES]
            zero = jnp.zeros_like(tile)
            qs = jnp.concatenate([jnp.where(first, tile, zero), jnp.where(first, zero, tile)], axis=0)
            s_scr[blk, tt] = _dot_nt(qs, k2) + bias_ref[0, tt]
        s = s_scr[blk]
        m = jnp.max(s, axis=-1, keepdims=True)
        p_scr[blk] = jnp.exp2(s - m).astype(BF16)
        for tt in range(tiles):
            pair = tt // GQA_GROUP
            v2 = jnp.concatenate([v[:, pair * LANES:(pair + 1) * LANES], ones], axis=1)
            on = _dot(p_scr[blk, tt], v2)
            sk = jnp.where(upper, sink_ref[order[2 * tt]], sink_ref[order[2 * tt + 1]]) * LOG2E
            o = on[:, :LANES] / (on[:, LANES:] + jnp.exp2(sk - m[tt]))
            o_ref[rows, tt * LANES:(tt + 1) * LANES] = \
                jnp.where(first, o[:BLOCK], o[BLOCK:]).astype(o_ref.dtype)


def _gqa(q, k, v, bias, sink, bsz, nb):
    t, qw = q.shape
    kw = k.shape[1]
    heads = qw // GQA_HEAD_DIM
    per = GQA_BLOCKS_PER_STEP
    assert nb % per == 0 and per >= 3, "query blocks are processed in groups with distinct first / last blocks"
    steps = nb // per
    cur = lambda w: pl.BlockSpec((per * BLOCK, w), lambda b, j: (b * steps + j, 0))
    prev = lambda w: pl.BlockSpec((BLOCK, w), lambda b, j: (b * nb + jnp.maximum(per * j - 1, 0), 0))
    nxt = lambda w: pl.BlockSpec((BLOCK, w), lambda b, j: (b * nb + jnp.minimum(per * (j + 1), nb - 1), 0))
    tile_bias = lambda variant: pl.BlockSpec((1, heads // 2, 2 * BLOCK, 3 * BLOCK),
                                             lambda b, j: (variant(j), 0, 0, 0))
    return pl.pallas_call(
        functools.partial(_gqa_kernel, heads=heads),
        grid=(bsz, steps),
        in_specs=[cur(qw), prev(kw), cur(kw), nxt(kw), prev(kw), cur(kw), nxt(kw),
                  tile_bias(lambda j: jnp.where(j == 0, 1, 0)),
                  tile_bias(lambda j: 0),
                  tile_bias(lambda j: jnp.where(j == steps - 1, 2, 0)),
                  pl.BlockSpec(memory_space=pltpu.SMEM)],
        out_specs=cur(qw),
        out_shape=jax.ShapeDtypeStruct((t, qw), BF16),
        scratch_shapes=[pltpu.VMEM((per, heads // 2, 2 * BLOCK, 3 * BLOCK), F32),
                        pltpu.VMEM((per, heads // 2, 2 * BLOCK, 3 * BLOCK), BF16)],
        compiler_params=pltpu.CompilerParams(dimension_semantics=("arbitrary", "arbitrary"),
                                             vmem_limit_bytes=BIG_VMEM_LIMIT),
        name="gqa_attn",
    )(q, k, k, k, v, v, v, bias, bias, bias, sink)


def _prepare(p):
    depth = p["norm_mix"].shape[0]
    row = lambda a: a.reshape(1, -1).astype(F32)
    prep = {"depth": depth, "layers": []}
    heads = p["gqa_sink"].shape[-1] if depth > 1 else 0
    if depth > 1:
        order = np.asarray(_gqa_head_order(heads))
        slot_cols = (order[:, None] * GQA_HEAD_DIM + np.arange(GQA_HEAD_DIM)[None, :]).reshape(-1)
        prep["t5_bias"] = _t5_bias(p["t5_table"].astype(F32))
    for layer in range(depth):
        i = layer // 2
        lp = {
            "ffn": [(row(p["norm_ffn"][layer, j]), p["w_ffn_gate"][layer, j].astype(BF16),
                     p["w_ffn_up"][layer, j].astype(BF16), p["w_ffn_down"][layer, j].astype(BF16))
                    for j in range(2)],
            "norm_mix": row(p["norm_mix"][layer]),
        }
        if layer % 2 == 0:
            lp["w_in"] = p["w_in_even"][i].astype(BF16)
            lp["tables"] = _s5_tables(p["s5_lam_re"][i].astype(F32), p["s5_lam_im"][i].astype(F32),
                                      p["s5_log_dt"][i].astype(F32), p["s5_b_re"][i].astype(F32),
                                      p["s5_b_im"][i].astype(F32), p["s5_c_re"][i].astype(F32),
                                      p["s5_c_im"][i].astype(F32))
            groups = p["s5_lam_re"].shape[2]
            lp["d_skip"] = jnp.repeat(p["s5_d"][i].astype(F32).reshape(groups, 1, S5_GROUP), CHUNK, axis=2)
            lp["w_glu"] = p["s5_w_glu"][i].astype(BF16)
            lp["b_glu"] = row(p["s5_b_glu"][i])
            lp["na_bias"] = _na_bias(p["na_rpb"][i].astype(F32))
            lp["w_out"] = p["w_out_even"][i].astype(BF16)
        else:
            w_in = p["w_in_odd"][i]
            nq = heads * GQA_HEAD_DIM
            lp["w_in"] = jnp.concatenate([w_in[:, slot_cols], w_in[:, nq:]], axis=1).astype(BF16)
            lp["sink"] = p["gqa_sink"][i].astype(F32)
            lp["w_out"] = p["w_out_odd"][i][slot_cols, :].astype(BF16)
        prep["layers"].append(lp)
    prep["norm_final"] = row(p["norm_final"])
    return prep


def _trunk(x, prep):
    bsz, seq, d = x.shape
    t = bsz * seq
    x = x.reshape(t, d).astype(F32)
    depth = prep["depth"]
    for layer, lp in enumerate(prep["layers"]):
        if layer % 2 == 0:
            x = _ffn(x, *lp["ffn"][0])
            u_cm, q, k, v = _inproj_even(x, lp["norm_mix"], lp["w_in"], NA_HEAD_DIM ** -0.5 * LOG2E)
            y_cm = _s5_apply(u_cm, lp["d_skip"], *lp["tables"], seq // CHUNK, bsz)
            na = _na(q, k, v, lp["na_bias"], bsz, seq // GRID_W)
            x = _out_even(y_cm, na, x, lp["w_glu"], lp["b_glu"], lp["w_out"])
            mixed = None
        else:
            heads = lp["sink"].shape[0]
            nq = heads * GQA_HEAD_DIM
            nkv = nq // GQA_GROUP
            scale = GQA_HEAD_DIM ** -0.5 * LOG2E
            x, q, k, v = _ffn(x, *lp["ffn"][0], project=(
                lp["norm_mix"], lp["w_in"], [(nq, BF16, scale), (nkv, BF16, 1.0), (nkv, BF16, 1.0)]))
            mixed = (_gqa(q, k, v, prep["t5_bias"], lp["sink"], bsz, seq // BLOCK), lp["w_out"])
        x = _ffn(x, *lp["ffn"][1], mixed=mixed, final_norm=prep["norm_final"] if layer == depth - 1 else None)
    return x.reshape(bsz, seq, d)


def kernel(x_prompt, x_sample, norm_ffn, w_ffn_gate, w_ffn_up, w_ffn_down, norm_mix, w_in_even, s5_lam_re, s5_lam_im, s5_log_dt, s5_b_re, s5_b_im, s5_c_re, s5_c_im, s5_d, s5_w_glu, s5_b_glu, na_rpb, w_out_even, w_in_odd, gqa_sink, w_out_odd, t5_table, norm_final):
    prep = _prepare(dict(
        norm_ffn=norm_ffn, w_ffn_gate=w_ffn_gate, w_ffn_up=w_ffn_up, w_ffn_down=w_ffn_down,
        norm_mix=norm_mix, w_in_even=w_in_even, s5_lam_re=s5_lam_re, s5_lam_im=s5_lam_im,
        s5_log_dt=s5_log_dt, s5_b_re=s5_b_re, s5_b_im=s5_b_im, s5_c_re=s5_c_re, s5_c_im=s5_c_im,
        s5_d=s5_d, s5_w_glu=s5_w_glu, s5_b_glu=s5_b_glu, na_rpb=na_rpb, w_out_even=w_out_even,
        w_in_odd=w_in_odd, gqa_sink=gqa_sink, w_out_odd=w_out_odd, t5_table=t5_table,
        norm_final=norm_final))
    return (_trunk(x_prompt, prep), _trunk(x_sample, prep))
```

```python
import functools
import math

import jax
import jax.numpy as jnp
import numpy as np
from jax import lax
from jax.experimental import pallas as pl
from jax.experimental.pallas import tpu as pltpu

F32 = jnp.float32
BF16 = jnp.bfloat16

S5_GROUP = 16
S5_STATE = 64
NA_HEAD_DIM = 64
NA_ROWS = 8
NA_COLS = 16
GRID_W = 64
GQA_HEAD_DIM = 64
GQA_GROUP = 4
WINDOW = 128
BLOCK = 128
T5_BUCKETS = 32
T5_MAX_DIST = 128
RMS_EPS = 1e-6
NEG_INF = -1e30
LOG2E = math.log2(math.e)

LANES = 128
CHUNK = LANES
EVEN_TOKEN_TILE = 8 * CHUNK
FFN_TOKEN_TILE = 1024
FFN_CHUNK = 512
FFN_ROW_SPLITS = 2
NA_ROWS_PER_STEP = 32
GQA_BLOCKS_PER_STEP = 4
VMEM_LIMIT = 48 * 1024 * 1024
BIG_VMEM_LIMIT = 60 * 1024 * 1024

_NT = (((1,), (1,)), ((), ()))
_HI = lax.Precision.HIGHEST


def _params(n_axes):
    return pltpu.CompilerParams(dimension_semantics=("arbitrary",) * n_axes,
                                vmem_limit_bytes=VMEM_LIMIT)


def _rms(x, g):
    return x * lax.rsqrt(jnp.mean(x * x, axis=-1, keepdims=True) + RMS_EPS) * g


def _dot(a, b):
    return jnp.dot(a, b, preferred_element_type=F32)


def _dot_nt(a, b, precision=None):
    return lax.dot_general(a, b, _NT, precision=precision, preferred_element_type=F32)


def _half_ffn(x, g_ref, wg_ref, wu_ref, wd_ref, bounds):
    outs = []
    for xs in jnp.split(x, FFN_ROW_SPLITS, axis=0):
        hn = _rms(xs, g_ref[...]).astype(BF16)
        acc = None
        for lo, hi in bounds:
            gate = _dot(hn, wg_ref[:, lo:hi])
            up = _dot(hn, wu_ref[:, lo:hi])
            h = (gate * jax.nn.sigmoid(gate) * up).astype(BF16)
            part = _dot(h, wd_ref[lo:hi, :])
            acc = part if acc is None else acc + part
        outs.append(xs + 0.5 * acc)
    return jnp.concatenate(outs, axis=0)


def _project(x, g_ref, w_ref, o_refs, scales):
    hn = _rms(x, g_ref[...]).astype(BF16)
    off = 0
    for o_ref, scale in zip(o_refs, scales):
        width = o_ref.shape[1]
        part = _dot(hn, w_ref[:, off:off + width])
        if scale != 1.0:
            part = part * scale
        o_ref[...] = part.astype(o_ref.dtype)
        off += width


def _ffn_kernel(x_ref, g_ref, wg_ref, wu_ref, wd_ref, gf_ref, o_ref, *, final_norm, bounds):
    y = _half_ffn(x_ref[...], g_ref, wg_ref, wu_ref, wd_ref, bounds)
    o_ref[...] = _rms(y, gf_ref[...]) if final_norm else y


def _ffn_inproj_kernel(x_ref, g_ref, wg_ref, wu_ref, wd_ref, g2_ref, win_ref, o_ref, *p_refs, bounds, scales):
    y = _half_ffn(x_ref[...], g_ref, wg_ref, wu_ref, wd_ref, bounds)
    o_ref[...] = y
    _project(y, g2_ref, win_ref, p_refs, scales)


def _proj_ffn_kernel(a_ref, wo_ref, x_ref, g_ref, wg_ref, wu_ref, wd_ref, gf_ref, o_ref, *, final_norm, bounds):
    x = x_ref[...] + _dot(a_ref[...], wo_ref[...])
    y = _half_ffn(x, g_ref, wg_ref, wu_ref, wd_ref, bounds)
    o_ref[...] = _rms(y, gf_ref[...]) if final_norm else y


def _ffn(x, g, wg, wu, wd, *, mixed=None, project=None, final_norm=None):
    t, d = x.shape
    f = wg.shape[1]
    tm = FFN_TOKEN_TILE
    bounds = tuple((lo, min(lo + FFN_CHUNK, f)) for lo in range(0, f, FFN_CHUNK))
    resident = lambda shape: pl.BlockSpec(shape, lambda i: (0, 0), pipeline_mode=pl.Buffered(1))
    row = lambda width: pl.BlockSpec((tm, width), lambda i: (i, 0))
    vec = pl.BlockSpec((1, d), lambda i: (0, 0))
    in_specs = [row(d), vec, resident((d, f)), resident((d, f)), resident((f, d))]
    args = (x, g, wg, wu, wd)
    out_specs = [row(d)]
    out_shape = [jax.ShapeDtypeStruct((t, d), F32)]
    final_gain = final_norm if final_norm is not None else g
    if project is not None:
        assert mixed is None and final_norm is None
        g2, w_in, splits = project
        in_specs += [vec, resident(w_in.shape)]
        args += (g2, w_in)
        out_specs += [row(width) for width, _, _ in splits]
        out_shape += [jax.ShapeDtypeStruct((t, width), dt) for width, dt, _ in splits]
        body = functools.partial(_ffn_inproj_kernel, bounds=bounds, scales=tuple(s for _, _, s in splits))
        name = "ffn_inproj"
    elif mixed is not None:
        a, w_out = mixed
        in_specs = [row(a.shape[1]), resident(w_out.shape)] + in_specs + [vec]
        args = (a, w_out) + args + (final_gain,)
        body = functools.partial(_proj_ffn_kernel, final_norm=final_norm is not None, bounds=bounds)
        name = "proj_ffn"
    else:
        in_specs += [vec]
        args += (final_gain,)
        body = functools.partial(_ffn_kernel, final_norm=final_norm is not None, bounds=bounds)
        name = "ffn"
    out = pl.pallas_call(
        body,
        grid=(t // tm,),
        in_specs=in_specs,
        out_specs=out_specs,
        out_shape=out_shape,
        compiler_params=pltpu.CompilerParams(dimension_semantics=("arbitrary",),
                                             vmem_limit_bytes=BIG_VMEM_LIMIT),
        name=name,
    )(*args)
    return out[0] if project is None else out


def _inproj_even_kernel(x_ref, g_ref, w_ref, ucm_ref, q_ref, k_ref, v_ref, *, scale):
    tm = x_ref.shape[0]
    half = q_ref.shape[1]
    n_chunks = tm // CHUNK
    hn = _rms(x_ref[...], g_ref[...]).astype(BF16)
    u = _dot(hn, w_ref[:, :half])
    ut = [u[ch * CHUNK:(ch + 1) * CHUNK, :].T for ch in range(n_chunks)]
    q_ref[...] = (_dot(hn, w_ref[:, half:2 * half]) * scale).astype(q_ref.dtype)
    k_ref[...] = _dot(hn, w_ref[:, 2 * half:3 * half]).astype(k_ref.dtype)
    v_ref[...] = _dot(hn, w_ref[:, 3 * half:]).astype(v_ref.dtype)
    for jb in range(half // 8):
        rows = jnp.swapaxes(jnp.stack([t[jb * 8:(jb + 1) * 8, :] for t in ut], axis=0), 0, 1)
        for r in range(8):
            grp, chan = divmod(jb * 8 + r, S5_GROUP)
            ucm_ref[grp, :, chan * CHUNK:(chan + 1) * CHUNK] = rows[r]


def _inproj_even(x, g, w, scale):
    t, d = x.shape
    half = w.shape[1] // 4
    groups = half // S5_GROUP
    tm = EVEN_TOKEN_TILE
    n_chunks = tm // CHUNK
    tok = lambda width: pl.BlockSpec((tm, width), lambda i: (i, 0))
    return pl.pallas_call(
        functools.partial(_inproj_even_kernel, scale=scale),
        grid=(t // tm,),
        in_specs=[tok(d), pl.BlockSpec((1, d), lambda i: (0, 0)), pl.BlockSpec(w.shape, lambda i: (0, 0))],
        out_specs=[pl.BlockSpec((groups, n_chunks, S5_GROUP * CHUNK), lambda i: (0, i, 0)),
                   tok(half), tok(half), tok(half)],
        out_shape=[jax.ShapeDtypeStruct((groups, t // CHUNK, S5_GROUP * CHUNK), F32)] +
                  [jax.ShapeDtypeStruct((t, half), BF16)] * 3,
        compiler_params=_params(1),
        name="inproj_even",
    )(x, g, w)


def _cmul(ar, ai, br, bi):
    return ar * br - ai * bi, ar * bi + ai * br


def _discretise(lr, li, dt):
    mag = jnp.exp(lr * dt)
    ar = mag * jnp.cos(li * dt)
    ai = mag * jnp.sin(li * dt)
    den = lr * lr + li * li
    nr = ar - 1.0
    return ar, ai, (nr * lr + ai * li) / den, (ai * lr - nr * li) / den


def _powers(exponent, ar, ai, shape):
    pr = jnp.ones(shape, F32)
    pi = jnp.zeros(shape, F32)
    sr, si = ar, ai
    for j in range((2 * CHUNK).bit_length() - 1):
        bit = ((exponent >> j) & 1) == 1
        mr, mi = _cmul(pr, pi, sr, si)
        pr = jnp.where(bit, mr, pr)
        pi = jnp.where(bit, mi, pi)
        sr, si = _cmul(sr, si, sr, si)
    keep = exponent >= 0
    return jnp.where(keep, pr, 0.0), jnp.where(keep, pi, 0.0)


def _s5_table_kernel(lre_ref, lim_ref, ldt_ref, lre_col_ref, lim_col_ref, btr_ref, bti_ref, cr_ref, ci_ref,
                     ctr_ref, cti_ref, m_ref, zb_ref, oc_ref, aq_ref, k2_scr):
    q = CHUNK
    p = S5_STATE
    c = S5_GROUP
    lane2 = lax.broadcasted_iota(jnp.int32, (1, 2 * q), 1)
    lane1 = lax.broadcasted_iota(jnp.int32, (1, q), 1)
    row1 = lax.broadcasted_iota(jnp.int32, (q, 1), 0)

    kappa = None
    for d in range(2):
        dt = jnp.exp(ldt_ref[d, 0])
        ar, ai, zr, zi = _discretise(lre_ref[d, 0], lim_ref[d, 0], dt)
        acr, aci, _, _ = _discretise(lre_col_ref[d, 0], lim_col_ref[d, 0], dt)
        btr = btr_ref[d, 0]
        bti = bti_ref[d, 0]
        bbr = zr * btr - zi * bti
        bbi = zr * bti + zi * btr
        cre = cr_ref[d, 0]
        cim = ci_ref[d, 0]

        rep = lambda a: jnp.concatenate([jnp.broadcast_to(a[i:i + 1], (c, p)) for i in range(c)], axis=0)
        til = lambda a: jnp.concatenate([a] * c, axis=0)
        wr, wi = _cmul(rep(bbr), rep(bbi), til(cre), til(cim))
        lag = (lane2 - q) if d == 0 else (q - lane2)
        pr, pi = _powers(lag, acr, aci, (p, 2 * q))
        part = jnp.dot(wr, pr, precision=_HI, preferred_element_type=F32) - \
            jnp.dot(wi, pi, precision=_HI, preferred_element_type=F32)
        kappa = part if kappa is None else kappa + part

        zpr, zpi = _powers((q - 1 - row1) if d == 0 else row1, ar, ai, (q, p))
        for i in range(c):
            zbr, zbi = _cmul(bbr[i:i + 1], bbi[i:i + 1], zpr, zpi)
            zb_ref[0, i * q:(i + 1) * q, d * p:(d + 1) * p] = zbr.astype(BF16)
            zb_ref[0, i * q:(i + 1) * q, 2 * p + d * p:2 * p + (d + 1) * p] = zbi.astype(BF16)

        opr, opi = _powers((lane1 + 1) if d == 0 else (q - lane1), acr, aci, (p, q))
        ctr = ctr_ref[d, 0]
        cti = cti_ref[d, 0]
        for i in range(c):
            e_re, e_im = _cmul(ctr[:, i:i + 1], cti[:, i:i + 1], opr, opi)
            oc_ref[0, d * p:(d + 1) * p, i * q:(i + 1) * q] = e_re.astype(BF16)
            oc_ref[0, 2 * p + d * p:2 * p + (d + 1) * p, i * q:(i + 1) * q] = (-e_im).astype(BF16)

        aqr, aqi = _powers(jnp.full((1, 1), q, jnp.int32), ar, ai, (1, p))
        aq_ref[0, :, d * p:(d + 1) * p] = aqr
        aq_ref[0, :, 2 * p + d * p:2 * p + (d + 1) * p] = aqi

    k2_scr[...] = kappa

    def fill(i, carry):
        for j in range(c):
            row = k2_scr[pl.ds(i * c + j, 1), :]
            toe = pltpu.roll(jnp.broadcast_to(row, (q, 2 * q)), q, axis=1, stride=1, stride_axis=0)
            m_ref[0, pl.ds(pl.multiple_of(i * q, q), q), j * q:(j + 1) * q] = toe[:, :q].astype(BF16)
        return carry

    lax.fori_loop(0, c, fill, 0)


def _s5_tables(lam_re, lam_im, log_dt, b_re, b_im, c_re, c_im):
    g = lam_re.shape[1]
    p, c, q = S5_STATE, S5_GROUP, CHUNK
    swap = lambda a: jnp.swapaxes(a, -1, -2)
    spec4 = lambda r, w: pl.BlockSpec((2, 1, r, w), lambda i: (0, i, 0, 0))
    return pl.pallas_call(
        _s5_table_kernel,
        grid=(g,),
        in_specs=[spec4(1, p), spec4(1, p), spec4(1, 1), spec4(p, 1), spec4(p, 1),
                  spec4(c, p), spec4(c, p), spec4(c, p), spec4(c, p), spec4(p, c), spec4(p, c)],
        out_specs=[
            pl.BlockSpec((1, q * c, q * c), lambda i: (i, 0, 0)),
            pl.BlockSpec((1, q * c, 4 * p), lambda i: (i, 0, 0)),
            pl.BlockSpec((1, 4 * p, q * c), lambda i: (i, 0, 0)),
            pl.BlockSpec((1, 1, 4 * p), lambda i: (i, 0, 0)),
        ],
        out_shape=[
            jax.ShapeDtypeStruct((g, q * c, q * c), BF16),
            jax.ShapeDtypeStruct((g, q * c, 4 * p), BF16),
            jax.ShapeDtypeStruct((g, 4 * p, q * c), BF16),
            jax.ShapeDtypeStruct((g, 1, 4 * p), F32),
        ],
        scratch_shapes=[pltpu.VMEM((c * c, 2 * q), F32)],
        compiler_params=_params(1),
        name="s5_tables",
    )(lam_re.reshape(2, g, 1, p), lam_im.reshape(2, g, 1, p), log_dt.reshape(2, g, 1, 1),
      lam_re.reshape(2, g, p, 1), lam_im.reshape(2, g, p, 1),
      swap(b_re), swap(b_im), c_re, c_im, swap(c_re), swap(c_im))


def _s5_kernel(u_ref, d_ref, m_ref, zb_ref, oc_ref, aq_ref, y_ref, z_scr, sf_scr, sb_scr, *, nc, bsz):
    half = 2 * S5_STATE
    uf = u_ref[0]
    u = uf.astype(BF16)
    z_scr[...] = jnp.swapaxes(_dot(u, zb_ref[0]).reshape(bsz, nc, 2 * half), 0, 1)
    aq = aq_ref[0]
    ar, ai = aq[:, :half], aq[:, half:]
    zero = jnp.zeros((bsz, half), F32)

    def step(j, carry):
        fr, fi, br, bi = carry
        sf_scr[j, :, :half] = fr
        sf_scr[j, :, half:] = fi
        sb_scr[nc - 1 - j, :, :half] = br
        sb_scr[nc - 1 - j, :, half:] = bi
        zf = z_scr[j]
        zb = z_scr[nc - 1 - j]
        nfr, nfi = _cmul(ar, ai, fr, fi)
        nbr, nbi = _cmul(ar, ai, br, bi)
        return nfr + zf[:, :half], nfi + zf[:, half:], nbr + zb[:, :half], nbi + zb[:, half:]

    lax.fori_loop(0, nc, step, (zero, zero, zero, zero))
    lane = lax.broadcasted_iota(jnp.int32, (1, 1, 2 * half), 2)
    s = jnp.where((lane & S5_STATE) == 0, sf_scr[...], sb_scr[...])
    s = jnp.swapaxes(s, 0, 1).reshape(bsz * nc, 2 * half).astype(BF16)
    y_ref[0] = d_ref[0] * uf + _dot(u, m_ref[0]) + _dot(s, oc_ref[0])


def _s5_apply(u_cm, d_cm, m, zb, oc, aq, nc, bsz):
    g, rows, w = u_cm.shape
    st = zb.shape[2]
    return pl.pallas_call(
        functools.partial(_s5_kernel, nc=nc, bsz=bsz),
        grid=(g,),
        in_specs=[
            pl.BlockSpec((1, rows, w), lambda i: (i, 0, 0)),
            pl.BlockSpec((1, 1, w), lambda i: (i, 0, 0)),
            pl.BlockSpec((1, w, w), lambda i: (i, 0, 0)),
            pl.BlockSpec((1, w, st), lambda i: (i, 0, 0)),
            pl.BlockSpec((1, st, w), lambda i: (i, 0, 0)),
            pl.BlockSpec((1, 1, st), lambda i: (i, 0, 0)),
        ],
        out_specs=pl.BlockSpec((1, rows, w), lambda i: (i, 0, 0)),
        out_shape=jax.ShapeDtypeStruct((g, rows, w), F32),
        scratch_shapes=[pltpu.VMEM((nc, bsz, st), F32)] * 3,
        compiler_params=pltpu.CompilerParams(dimension_semantics=("arbitrary",),
                                             vmem_limit_bytes=BIG_VMEM_LIMIT),
        name="s5_apply",
    )(u_cm, d_cm, m, zb, oc, aq)


def _na_bias_kernel(rpb_ref, o_ref):
    st = pl.program_id(0)
    h = pl.program_id(1)
    n_rel_rows = 2 * NA_ROWS - 1
    n_rel_cols = 2 * NA_COLS - 1
    w = lax.broadcasted_iota(jnp.int32, (GRID_W, LANES), 0)
    lane = lax.broadcasted_iota(jnp.int32, (GRID_W, LANES), 1)
    wk = lane & (GRID_W - 1)
    first = lane < GRID_W
    cs = jnp.clip(w - NA_COLS // 2, 0, GRID_W - NA_COLS)
    valid = (wk >= cs) & (wk < cs + NA_COLS)
    rel = wk - w + NA_COLS - 1
    for i in range(NA_ROWS * GRID_W // LANES):
        base0 = (h * n_rel_rows + st + 2 * i) * n_rel_cols
        base1 = base0 + n_rel_cols
        acc = jnp.full((GRID_W, LANES), NEG_INF, F32)
        for j in range(n_rel_cols):
            val = jnp.where(first, rpb_ref[base0 + j], rpb_ref[base1 + j]) * LOG2E
            acc = jnp.where(valid & (rel == j), val, acc)
        o_ref[0, 0, :, i * LANES:(i + 1) * LANES] = acc


def _na_bias(rpb):
    h = rpb.shape[0]
    pairs = h // 2
    return pl.pallas_call(
        _na_bias_kernel,
        grid=(NA_ROWS, h),
        in_specs=[pl.BlockSpec(memory_space=pltpu.SMEM)],
        out_specs=pl.BlockSpec((1, 1, GRID_W, NA_ROWS * GRID_W), lambda s, i: (s, i // 2, i % 2, 0)),
        out_shape=jax.ShapeDtypeStruct((NA_ROWS, pairs, 2 * GRID_W, NA_ROWS * GRID_W), F32),
        compiler_params=_params(2),
        name="na_bias",
    )(rpb.reshape(-1))


def _na_kernel(q_ref, k_ref, v_ref, b_ref, o_ref, s_scr, p_scr, *, rows):
    lane = lax.broadcasted_iota(jnp.int32, (GRID_W, LANES), 1)
    first = lane < NA_HEAD_DIM
    n_keys = NA_ROWS * GRID_W
    ones = jnp.ones((n_keys, LANES), BF16)

    def row_start(r):
        return jnp.clip(r - NA_ROWS // 2, 0, rows - NA_ROWS)

    def row_group(i, carry):
        base = i * NA_ROWS_PER_STEP
        for j in range(NA_ROWS_PER_STEP):
            r = base + j
            rs = row_start(r)
            q = q_ref[0, r]
            zero = jnp.zeros_like(q)
            qs = jnp.concatenate([jnp.where(first, q, zero), jnp.where(first, zero, q)], axis=0)
            kw = k_ref[0, pl.ds(rs, NA_ROWS)].reshape(n_keys, LANES)
            s_scr[j] = _dot_nt(qs, kw) + b_ref[rs - r + NA_ROWS - 1, 0]
        s = s_scr[...]
        p_scr[...] = jnp.exp2(s - jnp.max(s, axis=-1, keepdims=True)).astype(BF16)
        for j in range(NA_ROWS_PER_STEP):
            r = base + j
            vw = v_ref[0, pl.ds(row_start(r), NA_ROWS)].reshape(n_keys, LANES)
            on = _dot(p_scr[j], jnp.concatenate([vw, ones], axis=1))
            o = on[:, :LANES] / on[:, LANES:]
            o_ref[0, r] = jnp.where(first, o[:GRID_W], o[GRID_W:]).astype(o_ref.dtype)
        return carry

    lax.fori_loop(0, rows // NA_ROWS_PER_STEP, row_group, 0)


def _na(q, k, v, bias, bsz, rows):
    t, width = q.shape
    pairs = width // LANES
    shape4 = (bsz, rows, GRID_W, width)
    blk = pl.BlockSpec((1, rows, GRID_W, LANES), lambda b, i: (b, 0, 0, i))
    out = pl.pallas_call(
        functools.partial(_na_kernel, rows=rows),
        grid=(bsz, pairs),
        in_specs=[blk, blk, blk,
                  pl.BlockSpec((NA_ROWS, 1, 2 * GRID_W, NA_ROWS * GRID_W), lambda b, i: (0, i, 0, 0))],
        out_specs=blk,
        out_shape=jax.ShapeDtypeStruct(shape4, BF16),
        scratch_shapes=[pltpu.VMEM((NA_ROWS_PER_STEP, 2 * GRID_W, NA_ROWS * GRID_W), F32),
                        pltpu.VMEM((NA_ROWS_PER_STEP, 2 * GRID_W, NA_ROWS * GRID_W), BF16)],
        compiler_params=_params(2),
        name="na_attn",
    )(q.reshape(shape4), k.reshape(shape4), v.reshape(shape4), bias)
    return out.reshape(t, width)


def _out_even_kernel(ycm_ref, na_ref, x_ref, wglu_ref, bglu_ref, wo_ref, o_ref):
    groups, n_chunks, _ = ycm_ref.shape
    half = groups * S5_GROUP
    per_chunk = [[] for _ in range(n_chunks)]
    for jb in range(half // 8):
        tiles = []
        for r in range(8):
            grp, chan = divmod(jb * 8 + r, S5_GROUP)
            tiles.append(ycm_ref[grp, :, chan * CHUNK:(chan + 1) * CHUNK])
        rows = jnp.swapaxes(jnp.stack(tiles, axis=0), 0, 1)
        for ch in range(n_chunks):
            per_chunk[ch].append(rows[ch])
    for lo in range(0, n_chunks, n_chunks // 2):
        rows = slice(lo * CHUNK, (lo + n_chunks // 2) * CHUNK)
        y = jnp.concatenate([jnp.concatenate(blocks, axis=0).T
                             for blocks in per_chunk[lo:lo + n_chunks // 2]], axis=0)
        g = jax.nn.gelu(y)
        a = g * jax.nn.sigmoid(_dot(g.astype(BF16), wglu_ref[...]) + bglu_ref[...])
        mix = _dot(a.astype(BF16), wo_ref[:half, :]) + _dot(na_ref[rows, :], wo_ref[half:, :])
        o_ref[rows, :] = x_ref[rows, :] + mix


def _out_even(y_cm, na, x, w_glu, b_glu, w_out):
    t, d = x.shape
    groups = y_cm.shape[0]
    half = groups * S5_GROUP
    tm = EVEN_TOKEN_TILE
    n_chunks = tm // CHUNK
    row = lambda w: pl.BlockSpec((tm, w), lambda i: (i, 0))
    full = lambda r, w: pl.BlockSpec((r, w), lambda i: (0, 0))
    return pl.pallas_call(
        _out_even_kernel,
        grid=(t // tm,),
        in_specs=[pl.BlockSpec((groups, n_chunks, S5_GROUP * CHUNK), lambda i: (0, i, 0)),
                  row(half), row(d), full(half, half), full(1, half), full(d, d)],
        out_specs=row(d),
        out_shape=jax.ShapeDtypeStruct((t, d), F32),
        compiler_params=_params(1),
        name="out_even",
    )(y_cm, na, x, w_glu, b_glu, w_out)


def _t5_index_table():
    half = T5_BUCKETS // 2
    max_exact = half // 2
    rel = np.arange(3 * BLOCK)[None, :] - BLOCK - np.arange(BLOCK)[:, None]
    n = np.abs(rel)
    assert T5_MAX_DIST // max_exact == 16 and half - max_exact == 8 and max_exact == 8
    sq = np.maximum(n, 1).astype(np.int64) ** 2
    log2_sq = np.floor(np.log2(sq.astype(np.float64) + 0.5)).astype(np.int64)
    large = np.minimum(max_exact + log2_sq - 6, half - 1)
    bucket = np.where(rel > 0, half, 0) + np.where(n < max_exact, n, large)
    return np.where(n <= WINDOW, bucket, -1).astype(np.int32)


def _t5_bias_kernel(idx_ref, t5_ref, o_ref, *, heads):
    variant = pl.program_id(0)
    h = pl.program_id(1)
    idx = idx_ref[...]
    acc = jnp.full(idx.shape, NEG_INF, F32)
    for b in range(T5_BUCKETS):
        acc = jnp.where(idx == b, t5_ref[b * heads + h] * LOG2E, acc)
    col = lax.broadcasted_iota(jnp.int32, idx.shape, 1)
    absent = ((variant == 1) & (col < BLOCK)) | ((variant == 2) & (col >= 2 * BLOCK))
    o_ref[0, 0] = jnp.where(absent, NEG_INF, acc)


def _t5_bias(t5_table):
    heads = t5_table.shape[1]
    idx = jnp.asarray(_t5_index_table())
    per_pair = 2 * GQA_GROUP
    return pl.pallas_call(
        functools.partial(_t5_bias_kernel, heads=heads),
        grid=(3, heads),
        in_specs=[pl.BlockSpec((BLOCK, 3 * BLOCK), lambda e, h: (0, 0)),
                  pl.BlockSpec(memory_space=pltpu.SMEM)],
        out_specs=pl.BlockSpec(
            (1, 1, BLOCK, 3 * BLOCK),
            lambda e, h: (e, (h // per_pair) * GQA_GROUP + h % GQA_GROUP, (h % per_pair) // GQA_GROUP, 0)),
        out_shape=jax.ShapeDtypeStruct((3, heads // 2, 2 * BLOCK, 3 * BLOCK), F32),
        compiler_params=_params(2),
        name="t5_bias",
    )(idx, t5_table.reshape(-1))


def _gqa_head_order(heads):
    order = []
    for tt in range(heads // 2):
        pair, t = divmod(tt, GQA_GROUP)
        order += [2 * GQA_GROUP * pair + t, 2 * GQA_GROUP * pair + GQA_GROUP + t]
    return order


def _gqa_kernel(q_ref, kp_ref, kc_ref, kn_ref, vp_ref, vc_ref, vn_ref, bias_first_ref, bias_mid_ref, bias_last_ref,
                sink_ref, o_ref, s_scr, p_scr, *, heads):
    k_all = jnp.concatenate([kp_ref[...], kc_ref[...], kn_ref[...]], axis=0)
    v_all = jnp.concatenate([vp_ref[...], vc_ref[...], vn_ref[...]], axis=0)
    lane = lax.broadcasted_iota(jnp.int32, (BLOCK, LANES), 1)
    first = lane < GQA_HEAD_DIM
    upper = lax.broadcasted_iota(jnp.int32, (2 * BLOCK, 1), 0) < BLOCK
    ones = jnp.ones((3 * BLOCK, LANES), BF16)
    order = _gqa_head_order(heads)
    tiles = heads // 2
    per = GQA_BLOCKS_PER_STEP
    for blk in range(per):
        bias_ref = bias_first_ref if blk == 0 else bias_last_ref if blk == per - 1 else bias_mid_ref
        rows = slice(blk * BLOCK, (blk + 1) * BLOCK)
        k = k_all[blk * BLOCK:(blk + 3) * BLOCK]
        v = v_all[blk * BLOCK:(blk + 3) * BLOCK]
        for tt in range(tiles):
            k2 = k[:, tt // GQA_GROUP * LANES:(tt // GQA_GROUP + 1) * LANES]
            tile = q_ref[rows, tt * LANES:(tt + 1) * LANES]
            zero = jnp.zeros_like(tile)
            qs = jnp.concatenate([jnp.where(first, tile, zero), jnp.where(first, zero, tile)], axis=0)
            s_scr[blk, tt] = _dot_nt(qs, k2) + bias_ref[0, tt]
        s = s_scr[blk]
        m = jnp.max(s, axis=-1, keepdims=True)
        p_scr[blk] = jnp.exp2(s - m).astype(BF16)
        for tt in range(tiles):
            pair = tt // GQA_GROUP
            v2 = jnp.concatenate([v[:, pair * LANES:(pair + 1) * LANES], ones], axis=1)
            on = _dot(p_scr[blk, tt], v2)
            sk = jnp.where(upper, sink_ref[order[2 * tt]], sink_ref[order[2 * tt + 1]]) * LOG2E
            o = on[:, :LANES] / (on[:, LANES:] + jnp.exp2(sk - m[tt]))
            o_ref[rows, tt * LANES:(tt + 1) * LANES] = \
                jnp.where(first, o[:BLOCK], o[BLOCK:]).astype(o_ref.dtype)


def _gqa(q, k, v, bias, sink, bsz, nb):
    t, qw = q.shape
    kw = k.shape[1]
    heads = qw // GQA_HEAD_DIM
    per = GQA_BLOCKS_PER_STEP
    assert nb % per == 0 and per >= 3, "query blocks are processed in groups with distinct first / last blocks"
    steps = nb // per
    cur = lambda w: pl.BlockSpec((per * BLOCK, w), lambda b, j: (b * steps + j, 0))
    prev = lambda w: pl.BlockSpec((BLOCK, w), lambda b, j: (b * nb + jnp.maximum(per * j - 1, 0), 0))
    nxt = lambda w: pl.BlockSpec((BLOCK, w), lambda b, j: (b * nb + jnp.minimum(per * (j + 1), nb - 1), 0))
    tile_bias = lambda variant: pl.BlockSpec((1, heads // 2, 2 * BLOCK, 3 * BLOCK),
                                             lambda b, j: (variant(j), 0, 0, 0))
    return pl.pallas_call(
        functools.partial(_gqa_kernel, heads=heads),
        grid=(bsz, steps),
        in_specs=[cur(qw), prev(kw), cur(kw), nxt(kw), prev(kw), cur(kw), nxt(kw),
                  tile_bias(lambda j: jnp.where(j == 0, 1, 0)),
                  tile_bias(lambda j: 0),
                  tile_bias(lambda j: jnp.where(j == steps - 1, 2, 0)),
                  pl.BlockSpec(memory_space=pltpu.SMEM)],
        out_specs=cur(qw),
        out_shape=jax.ShapeDtypeStruct((t, qw), BF16),
        scratch_shapes=[pltpu.VMEM((per, heads // 2, 2 * BLOCK, 3 * BLOCK), F32),
                        pltpu.VMEM((per, heads // 2, 2 * BLOCK, 3 * BLOCK), BF16)],
        compiler_params=pltpu.CompilerParams(dimension_semantics=("arbitrary", "arbitrary"),
                                             vmem_limit_bytes=BIG_VMEM_LIMIT),
        name="gqa_attn",
    )(q, k, k, k, v, v, v, bias, bias, bias, sink)


def _prepare(p):
    depth = p["norm_mix"].shape[0]
    row = lambda a: a.reshape(1, -1).astype(F32)
    prep = {"depth": depth, "layers": []}
    heads = p["gqa_sink"].shape[-1] if depth > 1 else 0
    if depth > 1:
        order = np.asarray(_gqa_head_order(heads))
        slot_cols = (order[:, None] * GQA_HEAD_DIM + np.arange(GQA_HEAD_DIM)[None, :]).reshape(-1)
        prep["t5_bias"] = _t5_bias(p["t5_table"].astype(F32))
    for layer in range(depth):
        i = layer // 2
        lp = {
            "ffn": [(row(p["norm_ffn"][layer, j]), p["w_ffn_gate"][layer, j].astype(BF16),
                     p["w_ffn_up"][layer, j].astype(BF16), p["w_ffn_down"][layer, j].astype(BF16))
                    for j in range(2)],
            "norm_mix": row(p["norm_mix"][layer]),
        }
        if layer % 2 == 0:
            lp["w_in"] = p["w_in_even"][i].astype(BF16)
            lp["tables"] = _s5_tables(p["s5_lam_re"][i].astype(F32), p["s5_lam_im"][i].astype(F32),
                                      p["s5_log_dt"][i].astype(F32), p["s5_b_re"][i].astype(F32),
                                      p["s5_b_im"][i].astype(F32), p["s5_c_re"][i].astype(F32),
                                      p["s5_c_im"][i].astype(F32))
            groups = p["s5_lam_re"].shape[2]
            lp["d_skip"] = jnp.repeat(p["s5_d"][i].astype(F32).reshape(groups, 1, S5_GROUP), CHUNK, axis=2)
            lp["w_glu"] = p["s5_w_glu"][i].astype(BF16)
            lp["b_glu"] = row(p["s5_b_glu"][i])
            lp["na_bias"] = _na_bias(p["na_rpb"][i].astype(F32))
            lp["w_out"] = p["w_out_even"][i].astype(BF16)
        else:
            w_in = p["w_in_odd"][i]
            nq = heads * GQA_HEAD_DIM
            lp["w_in"] = jnp.concatenate([w_in[:, slot_cols], w_in[:, nq:]], axis=1).astype(BF16)
            lp["sink"] = p["gqa_sink"][i].astype(F32)
            lp["w_out"] = p["w_out_odd"][i][slot_cols, :].astype(BF16)
        prep["layers"].append(lp)
    prep["norm_final"] = row(p["norm_final"])
    return prep


def _trunk(x, prep):
    bsz, seq, d = x.shape
    t = bsz * seq
    x = x.reshape(t, d).astype(F32)
    depth = prep["depth"]
    for layer, lp in enumerate(prep["layers"]):
        if layer % 2 == 0:
            x = _ffn(x, *lp["ffn"][0])
            u_cm, q, k, v = _inproj_even(x, lp["norm_mix"], lp["w_in"], NA_HEAD_DIM ** -0.5 * LOG2E)
            y_cm = _s5_apply(u_cm, lp["d_skip"], *lp["tables"], seq // CHUNK, bsz)
            na = _na(q, k, v, lp["na_bias"], bsz, seq // GRID_W)
            x = _out_even(y_cm, na, x, lp["w_glu"], lp["b_glu"], lp["w_out"])
            mixed = None
        else:
            heads = lp["sink"].shape[0]
            nq = heads * GQA_HEAD_DIM
            nkv = nq // GQA_GROUP
            scale = GQA_HEAD_DIM ** -0.5 * LOG2E
            x, q, k, v = _ffn(x, *lp["ffn"][0], project=(
                lp["norm_mix"], lp["w_in"], [(nq, BF16, scale), (nkv, BF16, 1.0), (nkv, BF16, 1.0)]))
            mixed = (_gqa(q, k, v, prep["t5_bias"], lp["sink"], bsz, seq // BLOCK), lp["w_out"])
        x = _ffn(x, *lp["ffn"][1], mixed=mixed, final_norm=prep["norm_final"] if layer == depth - 1 else None)
    return x.reshape(bsz, seq, d)


def kernel(x_prompt, x_sample, norm_ffn, w_ffn_gate, w_ffn_up, w_ffn_down, norm_mix, w_in_even, s5_lam_re, s5_lam_im, s5_log_dt, s5_b_re, s5_b_im, s5_c_re, s5_c_im, s5_d, s5_w_glu, s5_b_glu, na_rpb, w_out_even, w_in_odd, gqa_sink, w_out_odd, t5_table, norm_final):
    prep = _prepare(dict(
        norm_ffn=norm_ffn, w_ffn_gate=w_ffn_gate, w_ffn_up=w_ffn_up, w_ffn_down=w_ffn_down,
        norm_mix=norm_mix, w_in_even=w_in_even, s5_lam_re=s5_lam_re, s5_lam_im=s5_lam_im,
        s5_log_dt=s5_log_dt, s5_b_re=s5_b_re, s5_b_im=s5_b_im, s5_c_re=s5_c_re, s5_c_im=s5_c_im,
        s5_d=s5_d, s5_w_glu=s5_w_glu, s5_b_glu=s5_b_glu, na_rpb=na_rpb, w_out_even=w_out_even,
        w_in_odd=w_in_odd, gqa_sink=gqa_sink, w_out_odd=w_out_odd, t5_table=t5_table,
        norm_final=norm_final))
    return (_trunk(x_prompt, prep), _trunk(x_sample, prep))
```
